```python
import jax, jax.numpy as jnp
from jax import lax
import numpy as np

D_MODEL = 2048
BATCH = 8
SEQ = 2048
DEPTH = 2
DEC_BATCH = 8
DEC_SEQ = 64
PAST_LEN = 2048

CHUNK = 64
D_MIX = D_MODEL
N_GROUPS = 4
G_WIDTH = D_MIX // N_GROUPS
SB_HEADS = 8
SB_HEAD_DIM = G_WIDTH // SB_HEADS
Q_BLOCK = 128
SCONV_W = 3
CCONV_W = 31
MLP_CHUNK = 128
MLP_HEADS = 8
MLP_HEAD_DIM = G_WIDTH // MLP_HEADS
N_IN = 10 * G_WIDTH
PEER_HEADS = 8
PEER_KEYS = 128
PEER_EXPERTS = PEER_KEYS * PEER_KEYS
PEER_TOPK = 16
PEER_QDIM = 256
PEER_HALF = PEER_QDIM // 2
PEER_BLOCK = 128
EPS = 1e-6

kernel_name = 'hybrid_stream_sb_conv_gmlp_peer_step'

F32 = jnp.float32


def _rmsnorm(x, g):
    xf = x.astype(F32)
    y = xf * lax.rsqrt(jnp.mean(xf * xf, axis=-1, keepdims=True) + EPS) * g.astype(F32)
    return y.astype(x.dtype)


def _layernorm(x, g, b):
    xf = x.astype(F32)
    xc = xf - jnp.mean(xf, axis=-1, keepdims=True)
    var = jnp.mean(xc * xc, axis=-1, keepdims=True)
    return (xc * lax.rsqrt(var + EPS) * g.astype(F32) + b.astype(F32)).astype(x.dtype)


def _sb_block(q, k, v, q_pos):
    z = jnp.einsum('bqhd,bkhd->bhqk', q.astype(F32), k.astype(F32)) * (SB_HEAD_DIM ** -0.5)
    mask = jnp.arange(k.shape[1])[None, :] < q_pos[:, None]
    log_fail = jnp.where(mask, jax.nn.log_sigmoid(-z), 0.0)
    later = lax.cumsum(log_fail, axis=3, reverse=True) - log_fail
    w = jnp.where(mask, jnp.exp(jax.nn.log_sigmoid(z) + later), 0.0)
    return jnp.einsum('bhqk,bkhd->bqhd', w, v.astype(F32)).astype(q.dtype)


def _sb_attention(q, k, v, offset):
    B, T, H, Dh = q.shape
    pos = offset + jnp.arange(T)
    if T <= Q_BLOCK:
        return _sb_block(q, k, v, pos)
    nb = T // Q_BLOCK
    qb = q.reshape(B, nb, Q_BLOCK, H, Dh).swapaxes(0, 1)
    out = lax.map(lambda a: _sb_block(a[0], k, v, a[1]), (qb, pos.reshape(nb, Q_BLOCK)))
    return out.swapaxes(0, 1).reshape(B, T, H, Dh)


def _causal_dwconv(x, past, w):
    xp = jnp.concatenate([past.astype(x.dtype), x], axis=1)
    y = lax.conv_general_dilated(xp, w[:, None, :].astype(x.dtype), window_strides=(1,),
                                 padding='VALID', dimension_numbers=('NWC', 'WIO', 'NWC'),
                                 feature_group_count=x.shape[-1])
    return y, xp[:, -(w.shape[0] - 1):]


def _spatial_gate(vn, w_s, b_s):
    B, T, C = vn.shape
    L = min(T, MLP_CHUNK)
    nc = T // L
    ws = jnp.tril(w_s[:, :L, :L]).astype(vn.dtype)
    vb = vn.reshape(B, nc, L, MLP_HEADS, MLP_HEAD_DIM)
    s = jnp.einsum('gts,bnsgc->bntgc', ws, vb) + b_s[:, :L].T[None, None, :, :, None].astype(vn.dtype)
    return s.reshape(B, T, C)


def _peer(h, w_query, keys1, keys2, exp_u, exp_v):
    B, T, D = h.shape
    n = B * T
    x = h.reshape(n, D)
    q = (x @ w_query).astype(F32).reshape(n, PEER_HEADS, 2, PEER_HALF)
    s1 = jnp.einsum('nhd,hkd->nhk', q[:, :, 0], keys1.astype(F32))
    s2 = jnp.einsum('nhd,hkd->nhk', q[:, :, 1], keys2.astype(F32))
    t1, i1 = lax.top_k(s1, PEER_TOPK)
    t2, i2 = lax.top_k(s2, PEER_TOPK)
    cand = (t1[..., :, None] + t2[..., None, :]).reshape(n, PEER_HEADS, PEER_TOPK * PEER_TOPK)
    cidx = (i1[..., :, None] * PEER_KEYS + i2[..., None, :]).reshape(n, PEER_HEADS, PEER_TOPK * PEER_TOPK)
    top, sel = lax.top_k(cand, PEER_TOPK)
    m = PEER_HEADS * PEER_TOPK
    eidx = jnp.take_along_axis(cidx, sel, axis=-1).reshape(n, m)
    gate = jax.nn.softmax(top, axis=-1).reshape(n, m)
    blk = min(PEER_BLOCK, n)
    nb = -(-n // blk)
    pad = nb * blk - n
    xb = jnp.pad(x, ((0, pad), (0, 0))).reshape(nb, blk, D)
    eb = jnp.pad(eidx, ((0, pad), (0, 0))).reshape(nb, blk, m)
    gb = jnp.pad(gate, ((0, pad), (0, 0))).reshape(nb, blk, m)

    def block(a):
        xs, es, gs = a
        act = jax.nn.gelu(jnp.einsum('nd,nmd->nm', xs, jnp.take(exp_u, es, axis=0),
                                     preferred_element_type=F32))
        return jnp.einsum('nm,nmd->nd', gs * act, jnp.take(exp_v, es, axis=0).astype(F32))

    out = lax.map(block, (xb, eb, gb)).reshape(nb * blk, D)[:n]
    return out.reshape(B, T, D).astype(h.dtype)


def _layer(x, k_past, v_past, sc_past, cc_past, offset,
           g_mix, w_in, w_sconv, w_cconv, b_cconv, g_cnorm, b_cnorm, g_vnorm, b_vnorm,
           w_spatial, b_spatial, w_out, g_ffn, w_query, keys1, keys2, exp_u, exp_v):
    B, T, _ = x.shape
    h = _rmsnorm(x, g_mix)
    q, k, v, sc_b, sc_c, sc_h, cf_a, cf_g, gm_u, gm_v = jnp.split(h @ w_in, 10, axis=-1)
    heads = (B, T, SB_HEADS, SB_HEAD_DIM)
    q, k, v = q.reshape(heads), k.reshape(heads), v.reshape(heads)
    k_all = k if k_past is None else jnp.concatenate([k_past.astype(k.dtype), k], axis=1)
    v_all = v if v_past is None else jnp.concatenate([v_past.astype(v.dtype), v], axis=1)
    y_a = _sb_attention(q, k_all, v_all, offset).reshape(B, T, G_WIDTH)
    conv_b, sc_state = _causal_dwconv(sc_c * sc_h, sc_past, w_sconv)
    y_b = sc_b * conv_b
    conv_c, cc_state = _causal_dwconv(cf_a * jax.nn.sigmoid(cf_g), cc_past, w_cconv)
    y_c = jax.nn.silu(_layernorm(conv_c + b_cconv.astype(conv_c.dtype), g_cnorm, b_cnorm))
    vn = _layernorm(jax.nn.gelu(gm_v), g_vnorm, b_vnorm)
    y_d = jax.nn.gelu(gm_u) * _spatial_gate(vn, w_spatial, b_spatial)
    x = x + jnp.concatenate([y_a, y_b, y_c, y_d], axis=-1) @ w_out
    x = x + _peer(_rmsnorm(x, g_ffn), w_query, keys1, keys2, exp_u, exp_v)
    return x, (k, v, sc_state, cc_state, vn)


def setup_inputs(seed: int = 0) -> dict:
    key = jax.random.key(seed)
    ks = jax.random.split(key, 26)

    def nrm(k, shape, s):
        return jax.random.normal(k, shape, F32) * s

    return {
        'x_prompt': nrm(ks[0], (BATCH, SEQ, D_MODEL), 1.0),
        'x_sample': nrm(ks[1], (DEC_BATCH, DEC_SEQ, D_MODEL), 1.0),
        'cache_k': nrm(ks[2], (DEPTH, DEC_BATCH, PAST_LEN, SB_HEADS, SB_HEAD_DIM), 1.0),
        'cache_v': nrm(ks[3], (DEPTH, DEC_BATCH, PAST_LEN, SB_HEADS, SB_HEAD_DIM), 1.0),
        'state_sconv': nrm(ks[4], (DEPTH, DEC_BATCH, SCONV_W - 1, G_WIDTH), 1.0),
        'state_cconv': nrm(ks[5], (DEPTH, DEC_BATCH, CCONV_W - 1, G_WIDTH), 0.5),
        'g_mix': 1.0 + nrm(ks[6], (DEPTH, D_MODEL), 0.02),
        'w_in': nrm(ks[7], (DEPTH, D_MODEL, N_IN), D_MODEL ** -0.5),
        'w_sconv': nrm(ks[8], (DEPTH, SCONV_W, G_WIDTH), SCONV_W ** -0.5),
        'w_cconv': nrm(ks[9], (DEPTH, CCONV_W, G_WIDTH), CCONV_W ** -0.5),
        'b_cconv': nrm(ks[10], (DEPTH, G_WIDTH), 0.02),
        'g_cnorm': 1.0 + nrm(ks[11], (DEPTH, G_WIDTH), 0.02),
        'b_cnorm': nrm(ks[12], (DEPTH, G_WIDTH), 0.02),
        'g_vnorm': 1.0 + nrm(ks[13], (DEPTH, G_WIDTH), 0.02),
        'b_vnorm': nrm(ks[14], (DEPTH, G_WIDTH), 0.02),
        'w_spatial': nrm(ks[15], (DEPTH, MLP_HEADS, MLP_CHUNK, MLP_CHUNK), MLP_CHUNK ** -0.5),
        'b_spatial': 1.0 + nrm(ks[16], (DEPTH, MLP_HEADS, MLP_CHUNK), 0.1),
        'w_out': nrm(ks[17], (DEPTH, D_MIX, D_MODEL), D_MIX ** -0.5),
        'g_ffn': 1.0 + nrm(ks[18], (DEPTH, D_MODEL), 0.02),
        'w_query': nrm(ks[19], (DEPTH, D_MODEL, PEER_HEADS * PEER_QDIM), D_MODEL ** -0.5),
        'sub_keys1': nrm(ks[20], (DEPTH, PEER_HEADS, PEER_KEYS, PEER_HALF), PEER_HALF ** -0.5),
        'sub_keys2': nrm(ks[21], (DEPTH, PEER_HEADS, PEER_KEYS, PEER_HALF), PEER_HALF ** -0.5),
        'expert_u': nrm(ks[22], (DEPTH, PEER_EXPERTS, D_MODEL), D_MODEL ** -0.5),
        'expert_v': nrm(ks[23], (DEPTH, PEER_EXPERTS, D_MODEL), PEER_HEADS ** -0.5),
        'g_final': 1.0 + nrm(ks[24], (D_MODEL,), 0.02),
    }


def reference(x_prompt, x_sample, cache_k, cache_v, state_sconv, state_cconv,
              g_mix, w_in, w_sconv, w_cconv, b_cconv, g_cnorm, b_cnorm, g_vnorm, b_vnorm,
              w_spatial, b_spatial, w_out, g_ffn, w_query, sub_keys1, sub_keys2,
              expert_u, expert_v, g_final):
    assert x_sample.shape[1] <= CHUNK
    xp, xs = x_prompt, x_sample
    nb_prompt = xp.shape[0]
    past_len = cache_k.shape[2]
    kp, vp, ksm, vsm, scp, scs, ccp, ccs, gvs = [], [], [], [], [], [], [], [], []
    for l in range(DEPTH):
        w = (g_mix[l], w_in[l], w_sconv[l], w_cconv[l], b_cconv[l], g_cnorm[l], b_cnorm[l],
             g_vnorm[l], b_vnorm[l], w_spatial[l], b_spatial[l], w_out[l], g_ffn[l],
             w_query[l], sub_keys1[l], sub_keys2[l], expert_u[l], expert_v[l])
        sc0 = jnp.zeros((nb_prompt, SCONV_W - 1, G_WIDTH), xp.dtype)
        cc0 = jnp.zeros((nb_prompt, CCONV_W - 1, G_WIDTH), xp.dtype)
        xp, (k1, v1, sc1, cc1, _) = _layer(xp, None, None, sc0, cc0, 0, *w)
        xs, (k2, v2, sc2, cc2, gv2) = _layer(xs, cache_k[l], cache_v[l], state_sconv[l],
                                             state_cconv[l], past_len, *w)
        kp.append(k1); vp.append(v1); scp.append(sc1); ccp.append(cc1)
        ksm.append(k2); vsm.append(v2); scs.append(sc2); ccs.append(cc2); gvs.append(gv2)
    y_prompt = _rmsnorm(xp, g_final)
    y_sample = _rmsnorm(xs, g_final)
    k_prompt, v_prompt = jnp.stack(kp), jnp.stack(vp)
    k_sample, v_sample = jnp.stack(ksm), jnp.stack(vsm)
    sconv_prompt, sconv_sample = jnp.stack(scp), jnp.stack(scs)
    cconv_prompt, cconv_sample = jnp.stack(ccp), jnp.stack(ccs)
    gmlp_v_sample = jnp.stack(gvs)
    return (y_prompt, y_sample, k_prompt, v_prompt, k_sample, v_sample,
            sconv_prompt, sconv_sample, cconv_prompt, cconv_sample, gmlp_v_sample)
```

```python
import functools

import jax
import jax.numpy as jnp
from jax import lax
from jax.experimental import pallas as pl
from jax.experimental.pallas import tpu as pltpu

F32 = jnp.float32
BF16 = jnp.bfloat16
EPS = 1e-6

G_WIDTH = 512
N_HEADS = 8
HEAD_DIM = 64
SCONV_W = 3
CCONV_W = 31
MLP_CHUNK = 128
PEER_HEADS = 8
PEER_KEYS = 128
PEER_TOPK = 16
PEER_HALF = 128
N_SEL = PEER_HEADS * PEER_TOPK

VMEM_LIMIT_BYTES = 56 * 1024 * 1024


def _params(*sem):
    return pltpu.CompilerParams(dimension_semantics=sem, vmem_limit_bytes=VMEM_LIMIT_BYTES)


def _gelu_tanh(x):
    return 0.5 * x * (1.0 + jnp.tanh(0.7978845608028654 * (x + 0.044715 * (x * x * x))))


def _layernorm(x, g, b):
    xc = x - jnp.mean(x, axis=-1, keepdims=True)
    var = jnp.mean(xc * xc, axis=-1, keepdims=True)
    return xc * lax.rsqrt(var + EPS) * g + b


def _rmsnorm_rows(x, g):
    return x * lax.rsqrt(jnp.mean(x * x, axis=-1, keepdims=True) + EPS) * g


def _norm_matmul_kernel(x_ref, g_ref, w_ref, o_ref, h_ref):
    @pl.when(pl.program_id(1) == 0)
    def _():
        h_ref[...] = _rmsnorm_rows(x_ref[...], g_ref[...]).astype(BF16)

    o_ref[...] = jnp.dot(h_ref[...], w_ref[...], preferred_element_type=F32)


def _norm_matmul(x, g, w, *, tm, tn):
    n, d = x.shape
    nout = w.shape[1]
    return pl.pallas_call(
        _norm_matmul_kernel,
        grid=(n // tm, nout // tn),
        in_specs=[
            pl.BlockSpec((tm, d), lambda i, j: (i, 0)),
            pl.BlockSpec((1, d), lambda i, j: (0, 0)),
            pl.BlockSpec((d, tn), lambda i, j: (0, j)),
        ],
        out_specs=pl.BlockSpec((tm, tn), lambda i, j: (i, j)),
        out_shape=jax.ShapeDtypeStruct((n, nout), F32),
        scratch_shapes=[pltpu.VMEM((tm, d), BF16)],
        compiler_params=_params("parallel", "arbitrary"),
        name="norm_matmul",
    )(x, g.reshape(1, d), w)


def _sb_kernel(q_ref, k_ref, v_ref, o_ref, run_ref, *, tq, tk, q_offset):
    qi = pl.program_id(1)
    q_pos0 = q_offset + qi * tq
    n_kb = jnp.minimum(pl.cdiv(q_pos0 + tq, tk), k_ref.shape[1] // tk)

    run_ref[...] = jnp.zeros_like(run_ref)
    o_ref[...] = jnp.zeros_like(o_ref)

    row = lax.broadcasted_iota(jnp.int32, (tq, tk), 0)
    col = lax.broadcasted_iota(jnp.int32, (tk, 2 * tk), 1)
    krow = lax.broadcasted_iota(jnp.int32, (tk, 2 * tk), 0)
    suffix_and_total = jnp.where((krow >= col) | (col >= tk), 1.0, 0.0).astype(BF16)
    col_minus_row = lax.broadcasted_iota(jnp.int32, (tq, tk), 1) - row

    def body(it, carry):
        kb = n_kb - 1 - it
        k0 = pl.multiple_of(kb * tk, tk)
        mask = col_minus_row < (q_pos0 - k0)
        for h in range(N_HEADS):
            hs = slice(h * HEAD_DIM, (h + 1) * HEAD_DIM)
            qh = (q_ref[0, :, hs] * (HEAD_DIM ** -0.5)).astype(BF16)
            kh = k_ref[0, pl.ds(k0, tk), hs].astype(BF16)
            vh = v_ref[0, pl.ds(k0, tk), hs].astype(BF16)
            z = lax.dot_general(qh, kh, (((1,), (1,)), ((), ())), preferred_element_type=F32)
            lf = -(jnp.maximum(z, 0.0) + jnp.log1p(jnp.exp(-jnp.abs(z))))
            lf = jnp.where(mask, lf, 0.0)
            lf_hi = lf.astype(BF16)
            lf_lo = (lf - lf_hi.astype(F32)).astype(BF16)
            sums = (jnp.dot(lf_hi, suffix_and_total, preferred_element_type=F32)
                    + jnp.dot(lf_lo, suffix_and_total, preferred_element_type=F32))
            run = run_ref[h]
            w = jnp.where(mask, jnp.exp(z + sums[:, :tk] + run), 0.0)
            o_ref[0, :, hs] += jnp.dot(w.astype(BF16), vh, preferred_element_type=F32)
            run_ref[h] = run + sums[:, tk:]
        return carry

    lax.fori_loop(0, n_kb, body, 0)


def _sb_attention(q_arr, q_col, k_arr, k_col, v_arr, v_col, *, t_q, tq, tk, q_offset):
    b = q_arr.shape[0]
    t_k = k_arr.shape[1]
    kern = functools.partial(_sb_kernel, tq=tq, tk=tk, q_offset=q_offset)
    return pl.pallas_call(
        kern,
        grid=(b, t_q // tq),
        in_specs=[
            pl.BlockSpec((1, tq, G_WIDTH), lambda i, j: (i, j, q_col)),
            pl.BlockSpec((1, t_k, G_WIDTH), lambda i, j: (i, 0, k_col)),
            pl.BlockSpec((1, t_k, G_WIDTH), lambda i, j: (i, 0, v_col)),
        ],
        out_specs=pl.BlockSpec((1, tq, G_WIDTH), lambda i, j: (i, j, 0)),
        out_shape=jax.ShapeDtypeStruct((b, t_q, G_WIDTH), F32),
        scratch_shapes=[pltpu.VMEM((N_HEADS, tq, tk), F32)],
        compiler_params=_params("parallel", "arbitrary"),
        name="sb_attention",
    )(q_arr, k_arr, v_arr)


SC_HIST = 8
CC_HIST = 32


def _mixers_kernel(scb_ref, scc_ref, sch_ref, cfa_ref, cfg_ref, gmu_ref, gmv_ref,
                   sc_past_ref, cc_past_ref, wsc_ref, wcc_ref, bcc_ref, gcn_ref, bcn_ref,
                   gvn_ref, bvn_ref, wsp_ref, bsp_ref,
                   y_ref, vn_ref, sc_state_ref, cc_state_ref, u_buf, g_buf, *, tt, chunk):
    ti = pl.program_id(1)

    @pl.when(ti == 0)
    def _():
        u_buf[0:SC_HIST, :] = jnp.zeros((SC_HIST, G_WIDTH), F32)
        g_buf[0:CC_HIST, :] = jnp.zeros((CC_HIST, G_WIDTH), F32)
        u_buf[SC_HIST - (SCONV_W - 1):SC_HIST, :] = sc_past_ref[0]
        g_buf[CC_HIST - (CCONV_W - 1):CC_HIST, :] = cc_past_ref[0]

    u_buf[SC_HIST:SC_HIST + tt, :] = scc_ref[0] * sch_ref[0]
    conv_b = jnp.zeros((tt, G_WIDTH), F32)
    for k in range(SCONV_W):
        s0 = SC_HIST - (SCONV_W - 1) + k
        conv_b = conv_b + wsc_ref[k:k + 1, :] * u_buf[s0:s0 + tt, :]
    y_ref[0, :, 0:G_WIDTH] = scb_ref[0] * conv_b

    g_buf[CC_HIST:CC_HIST + tt, :] = cfa_ref[0] * jax.nn.sigmoid(cfg_ref[0])
    conv_c = jnp.zeros((tt, G_WIDTH), F32)
    for k in range(CCONV_W):
        s0 = CC_HIST - (CCONV_W - 1) + k
        conv_c = conv_c + wcc_ref[k:k + 1, :] * g_buf[s0:s0 + tt, :]
    ln = _layernorm(conv_c + bcc_ref[...], gcn_ref[...], bcn_ref[...])
    y_ref[0, :, G_WIDTH:2 * G_WIDTH] = ln * jax.nn.sigmoid(ln)

    vn = _layernorm(_gelu_tanh(gmv_ref[0]), gvn_ref[...], bvn_ref[...])
    vn_ref[0] = vn
    vn16 = vn.astype(BF16)
    for c in range(tt // chunk):
        rows = slice(c * chunk, (c + 1) * chunk)
        parts = []
        for h in range(N_HEADS):
            hs = slice(h * HEAD_DIM, (h + 1) * HEAD_DIM)
            parts.append(jnp.dot(wsp_ref[h], vn16[rows, hs], preferred_element_type=F32))
        s = jnp.concatenate(parts, axis=-1) + bsp_ref[...]
        y_ref[0, rows, 2 * G_WIDTH:3 * G_WIDTH] = _gelu_tanh(gmu_ref[0, rows, :]) * s

    sc_state_ref[0] = u_buf[SC_HIST + tt - (SCONV_W - 1):SC_HIST + tt, :]
    cc_state_ref[0] = g_buf[CC_HIST + tt - (CCONV_W - 1):CC_HIST + tt, :]
    u_buf[0:SC_HIST, :] = u_buf[tt:tt + SC_HIST, :]
    g_buf[0:CC_HIST, :] = g_buf[tt:tt + CC_HIST, :]


def _mixers(proj3, sc_past, cc_past, w_sconv, w_cconv, b_cconv, g_cnorm, b_cnorm, g_vnorm, b_vnorm,
            w_spatial, b_spatial, *, tt):
    b, t, _ = proj3.shape
    chunk = min(t, MLP_CHUNK)
    wsp = jnp.tril(w_spatial[:, :chunk, :chunk]).astype(BF16)
    bsp = jnp.repeat(b_spatial[:, :chunk].T, HEAD_DIM, axis=1)

    def col(c):
        return pl.BlockSpec((1, tt, G_WIDTH), lambda i, j: (i, j, c))

    def whole(a):
        return pl.BlockSpec(a.shape, lambda i, j: (0,) * a.ndim)

    def per_batch(a):
        return pl.BlockSpec((1,) + a.shape[1:], lambda i, j: (i,) + (0,) * (a.ndim - 1))

    row = lambda a: a.reshape(1, G_WIDTH)
    small = [w_sconv, w_cconv, row(b_cconv), row(g_cnorm), row(b_cnorm), row(g_vnorm), row(b_vnorm), wsp, bsp]
    kern = functools.partial(_mixers_kernel, tt=tt, chunk=chunk)
    return pl.pallas_call(
        kern,
        grid=(b, t // tt),
        in_specs=[col(c) for c in range(3, 10)] + [per_batch(sc_past), per_batch(cc_past)]
                 + [whole(a) for a in small],
        out_specs=[
            pl.BlockSpec((1, tt, 3 * G_WIDTH), lambda i, j: (i, j, 0)),
            pl.BlockSpec((1, tt, G_WIDTH), lambda i, j: (i, j, 0)),
            pl.BlockSpec((1, SCONV_W - 1, G_WIDTH), lambda i, j: (i, 0, 0)),
            pl.BlockSpec((1, CCONV_W - 1, G_WIDTH), lambda i, j: (i, 0, 0)),
        ],
        out_shape=[
            jax.ShapeDtypeStruct((b, t, 3 * G_WIDTH), F32),
            jax.ShapeDtypeStruct((b, t, G_WIDTH), F32),
            jax.ShapeDtypeStruct((b, SCONV_W - 1, G_WIDTH), F32),
            jax.ShapeDtypeStruct((b, CCONV_W - 1, G_WIDTH), F32),
        ],
        scratch_shapes=[pltpu.VMEM((SC_HIST + tt, G_WIDTH), F32), pltpu.VMEM((CC_HIST + tt, G_WIDTH), F32)],
        compiler_params=_params("parallel", "arbitrary"),
        name="mixers",
    )(*([proj3] * 7), sc_past, cc_past, *small)


def _out_proj_kernel(x_ref, ya_ref, yb_ref, wa_ref, wb_ref, o_ref):
    o_ref[...] = (x_ref[...]
                  + jnp.dot(ya_ref[...].astype(BF16), wa_ref[...], preferred_element_type=F32)
                  + jnp.dot(yb_ref[...].astype(BF16), wb_ref[...], preferred_element_type=F32))


def _out_proj(x, ya, yb, w_out, *, tm):
    n, d = x.shape
    wa = w_out[:G_WIDTH].astype(BF16)
    wb = w_out[G_WIDTH:].astype(BF16)
    return pl.pallas_call(
        _out_proj_kernel,
        grid=(n // tm,),
        in_specs=[
            pl.BlockSpec((tm, d), lambda i: (i, 0)),
            pl.BlockSpec((tm, G_WIDTH), lambda i: (i, 0)),
            pl.BlockSpec((tm, 3 * G_WIDTH), lambda i: (i, 0)),
            pl.BlockSpec(wa.shape, lambda i: (0, 0)),
            pl.BlockSpec(wb.shape, lambda i: (0, 0)),
        ],
        out_specs=pl.BlockSpec((tm, d), lambda i: (i, 0)),
        out_shape=jax.ShapeDtypeStruct((n, d), F32),
        compiler_params=_params("parallel"),
        name="out_proj",
    )(x, ya, yb, wa, wb)


def _top16_rows(s, n_rows):
    rid = lax.broadcasted_iota(jnp.int32, s.shape, 0).astype(F32)
    vals, idxs = [], []
    for _ in range(PEER_TOPK):
        m = jnp.max(s, axis=0, keepdims=True)
        i = jnp.min(jnp.where(s == m, rid, float(n_rows)), axis=0, keepdims=True)
        s = jnp.where(rid == i, -jnp.inf, s)
        vals.append(m)
        idxs.append(i)
    return jnp.concatenate(vals, axis=0), jnp.concatenate(idxs, axis=0)


def _peer_topk_kernel(q_ref, k1_ref, k2_ref, eidx_ref, gate_ref):
    nt = (((1,), (1,)), ((), ()))
    e_rows, g_rows = [], []
    for h in range(PEER_HEADS):
        qa = q_ref[:, (2 * h) * PEER_HALF:(2 * h + 1) * PEER_HALF].astype(BF16)
        qb = q_ref[:, (2 * h + 1) * PEER_HALF:(2 * h + 2) * PEER_HALF].astype(BF16)
        s1 = lax.dot_general(k1_ref[h].astype(BF16), qa, nt, preferred_element_type=F32)
        s2 = lax.dot_general(k2_ref[h].astype(BF16), qb, nt, preferred_element_type=F32)
        t1, i1 = _top16_rows(s1, PEER_KEYS)
        t2, i2 = _top16_rows(s2, PEER_KEYS)
        cand = jnp.concatenate([t1[a:a + 1] + t2 for a in range(PEER_TOPK)], axis=0)
        cidx = jnp.concatenate([i1[a:a + 1] * float(PEER_KEYS) + i2 for a in range(PEER_TOPK)], axis=0)
        top, sel = _top16_rows(cand, PEER_TOPK * PEER_TOPK)
        rid = lax.broadcasted_iota(jnp.int32, cand.shape, 0).astype(F32)
        eidx = [jnp.sum(jnp.where(rid == sel[r:r + 1], cidx, 0.0), axis=0, keepdims=True)
                for r in range(PEER_TOPK)]
        p = jnp.exp(top - top[0:1])
        g_rows.append(p / jnp.sum(p, axis=0, keepdims=True))
        e_rows.append(jnp.concatenate(eidx, axis=0))
    eidx_ref[...] = jnp.concatenate(e_rows, axis=0).T.astype(jnp.int32)
    gate_ref[...] = jnp.concatenate(g_rows, axis=0).T


def _peer_topk(q, keys1, keys2, *, tn):
    n, d = q.shape
    return pl.pallas_call(
        _peer_topk_kernel,
        grid=(n // tn,),
        in_specs=[
            pl.BlockSpec((tn, d), lambda i: (i, 0)),
            pl.BlockSpec(keys1.shape, lambda i: (0, 0, 0)),
            pl.BlockSpec(keys2.shape, lambda i: (0, 0, 0)),
        ],
        out_specs=[pl.BlockSpec((tn, N_SEL), lambda i: (i, 0)), pl.BlockSpec((tn, N_SEL), lambda i: (i, 0))],
        out_shape=[jax.ShapeDtypeStruct((n, N_SEL), jnp.int32), jax.ShapeDtypeStruct((n, N_SEL), F32)],
        compiler_params=_params("parallel"),
        name="peer_topk",
    )(q, keys1, keys2)


N_SLOTS = 3


def _peer_experts_kernel(eidx_ref, x_ref, gate_ref, g_ref, gf_ref, u_hbm, v_hbm, o_ref,
                         ubuf, vbuf, usem, vsem, *, tb, final_norm):
    def issue(t, slot):
        for m in range(N_SEL):
            e = eidx_ref[t, m]
            pltpu.make_async_copy(u_hbm.at[pl.ds(e, 1), :], ubuf.at[slot, pl.ds(m, 1), :], usem.at[slot]).start()
            pltpu.make_async_copy(v_hbm.at[pl.ds(e, 1), :], vbuf.at[slot, pl.ds(m, 1), :], vsem.at[slot]).start()

    def wait(slot):
        pltpu.make_async_copy(u_hbm.at[pl.ds(0, N_SEL), :], ubuf.at[slot], usem.at[slot]).wait()
        pltpu.make_async_copy(v_hbm.at[pl.ds(0, N_SEL), :], vbuf.at[slot], vsem.at[slot]).wait()

    eye = (lax.broadcasted_iota(jnp.int32, (N_SEL, N_SEL), 0)
           == lax.broadcasted_iota(jnp.int32, (N_SEL, N_SEL), 1))

    def compute(t, slot):
        x = x_ref[pl.ds(t, 1), :]
        h = _rmsnorm_rows(x, g_ref[...])
        act = _gelu_tanh(jnp.sum(ubuf[slot] * h, axis=-1, keepdims=True))
        gate_col = jnp.sum(jnp.where(eye, gate_ref[pl.ds(t, 1), :], 0.0), axis=-1, keepdims=True)
        y = x + jnp.sum((gate_col * act) * vbuf[slot], axis=0, keepdims=True)
        if final_norm:
            y = _rmsnorm_rows(y, gf_ref[...])
        o_ref[pl.ds(t, 1), :] = y

    for t in range(N_SLOTS - 1):
        issue(t, t)

    def body(t, carry):
        slot = lax.rem(t, N_SLOTS)
        wait(slot)

        @pl.when(t + (N_SLOTS - 1) < tb)
        def _():
            issue(t + (N_SLOTS - 1), lax.rem(t + (N_SLOTS - 1), N_SLOTS))

        compute(t, slot)
        return carry

    lax.fori_loop(0, tb, body, 0)


def _peer_experts(x, eidx, gate, g_ffn, g_final, exp_u, exp_v, *, tb, final_norm):
    n, d = x.shape
    kern = functools.partial(_peer_experts_kernel, tb=tb, final_norm=final_norm)
    return pl.pallas_call(
        kern,
        grid=(n // tb,),
        in_specs=[
            pl.BlockSpec((tb, N_SEL), lambda i: (i, 0), memory_space=pltpu.SMEM),
            pl.BlockSpec((tb, d), lambda i: (i, 0)),
            pl.BlockSpec((tb, N_SEL), lambda i: (i, 0)),
            pl.BlockSpec((1, d), lambda i: (0, 0)),
            pl.BlockSpec((1, d), lambda i: (0, 0)),
            pl.BlockSpec(memory_space=pl.ANY),
            pl.BlockSpec(memory_space=pl.ANY),
        ],
        out_specs=pl.BlockSpec((tb, d), lambda i: (i, 0)),
        out_shape=jax.ShapeDtypeStruct((n, d), F32),
        scratch_shapes=[
            pltpu.VMEM((N_SLOTS, N_SEL, d), F32),
            pltpu.VMEM((N_SLOTS, N_SEL, d), F32),
            pltpu.SemaphoreType.DMA((N_SLOTS,)),
            pltpu.SemaphoreType.DMA((N_SLOTS,)),
        ],
        compiler_params=_params("arbitrary"),
        name="peer_experts",
    )(eidx, x, gate, g_ffn.reshape(1, d), g_final.reshape(1, d), exp_u, exp_v)


def _layer(x3, k_past, v_past, sc_past, cc_past, w, g_final, final_norm):
    (g_mix, w_in16, w_sconv, w_cconv, b_cconv, g_cnorm, b_cnorm, g_vnorm, b_vnorm,
     w_spatial, b_spatial, w_out, g_ffn, w_query16, keys1, keys2, exp_u, exp_v) = w
    b, t, d = x3.shape
    n = b * t
    x = x3.reshape(n, d)
    tm = min(512, n)

    proj = _norm_matmul(x, g_mix, w_in16, tm=tm, tn=1280)
    proj3 = proj.reshape(b, t, proj.shape[1])
    k_new = proj3[:, :, G_WIDTH:2 * G_WIDTH]
    v_new = proj3[:, :, 2 * G_WIDTH:3 * G_WIDTH]
    if k_past is None:
        y_a = _sb_attention(proj3, 0, proj3, 1, proj3, 2, t_q=t, tq=128, tk=128, q_offset=0)
    else:
        past = k_past.shape[1]
        k_all = jnp.concatenate([k_past.reshape(b, past, G_WIDTH), k_new], axis=1)
        v_all = jnp.concatenate([v_past.reshape(b, past, G_WIDTH), v_new], axis=1)
        y_a = _sb_attention(proj3, 0, k_all, 0, v_all, 0, t_q=t, tq=t, tk=t, q_offset=past)
    y_bcd, vn, sc_state, cc_state = _mixers(
        proj3, sc_past, cc_past, w_sconv, w_cconv, b_cconv, g_cnorm, b_cnorm, g_vnorm, b_vnorm,
        w_spatial, b_spatial, tt=min(512, t))
    x1 = _out_proj(x, y_a.reshape(n, G_WIDTH), y_bcd.reshape(n, 3 * G_WIDTH), w_out, tm=min(256, n))
    q = _norm_matmul(x1, g_ffn, w_query16, tm=tm, tn=1024)
    eidx, gate = _peer_topk(q, keys1, keys2, tn=min(256, n))
    x2 = _peer_experts(x1, eidx, gate, g_ffn, g_final, exp_u, exp_v, tb=min(64, n), final_norm=final_norm)
    heads = (b, t, N_HEADS, HEAD_DIM)
    return x2.reshape(b, t, d), (k_new.reshape(heads), v_new.reshape(heads), sc_state, cc_state, vn)


def kernel(x_prompt, x_sample, cache_k, cache_v, state_sconv, state_cconv, g_mix, w_in, w_sconv, w_cconv,
           b_cconv, g_cnorm, b_cnorm, g_vnorm, b_vnorm, w_spatial, b_spatial, w_out, g_ffn, w_query,
           sub_keys1, sub_keys2, expert_u, expert_v, g_final):
    depth = w_in.shape[0]
    xp, xs = x_prompt, x_sample
    nb = xp.shape[0]
    outs = [[] for _ in range(9)]
    for l in range(depth):
        w = (g_mix[l], w_in[l].astype(BF16), w_sconv[l], w_cconv[l], b_cconv[l], g_cnorm[l], b_cnorm[l],
             g_vnorm[l], b_vnorm[l], w_spatial[l], b_spatial[l], w_out[l], g_ffn[l],
             w_query[l].astype(BF16), sub_keys1[l], sub_keys2[l], expert_u[l], expert_v[l])
        last = l == depth - 1
        sc0 = jnp.zeros((nb, SCONV_W - 1, G_WIDTH), F32)
        cc0 = jnp.zeros((nb, CCONV_W - 1, G_WIDTH), F32)
        xp, (k1, v1, sc1, cc1, _) = _layer(xp, None, None, sc0, cc0, w, g_final, last)
        xs, (k2, v2, sc2, cc2, gv2) = _layer(xs, cache_k[l], cache_v[l], state_sconv[l], state_cconv[l],
                                             w, g_final, last)
        for lst, val in zip(outs, (k1, v1, k2, v2, sc1, sc2, cc1, cc2, gv2)):
            lst.append(val)
    kp, vp, ksm, vsm, scp, scs, ccp, ccs, gvs = [jnp.stack(o) for o in outs]
    return (xp, xs, kp, vp, ksm, vsm, scp, scs, ccp, ccs, gvs)
```

```python
import functools

import jax
import jax.numpy as jnp
from jax import lax
from jax.experimental import pallas as pl
from jax.experimental.pallas import tpu as pltpu

F32 = jnp.float32
BF16 = jnp.bfloat16
EPS = 1e-6

G_WIDTH = 512
N_HEADS = 8
HEAD_DIM = 64
SCONV_W = 3
CCONV_W = 31
MLP_CHUNK = 128
PEER_HEADS = 8
PEER_KEYS = 128
PEER_TOPK = 16
PEER_HALF = 128
N_SEL = PEER_HEADS * PEER_TOPK

VMEM_LIMIT_BYTES = 56 * 1024 * 1024


def _params(*sem):
    return pltpu.CompilerParams(dimension_semantics=sem, vmem_limit_bytes=VMEM_LIMIT_BYTES)


def _gelu_tanh(x):
    return 0.5 * x * (1.0 + jnp.tanh(0.7978845608028654 * (x + 0.044715 * (x * x * x))))


def _layernorm(x, g, b):
    xc = x - jnp.mean(x, axis=-1, keepdims=True)
    var = jnp.mean(xc * xc, axis=-1, keepdims=True)
    return xc * lax.rsqrt(var + EPS) * g + b


def _rmsnorm_rows(x, g):
    return x * lax.rsqrt(jnp.mean(x * x, axis=-1, keepdims=True) + EPS) * g


def _norm_matmul_kernel(x_ref, g_ref, w_ref, o_ref, h_ref):
    @pl.when(pl.program_id(1) == 0)
    def _():
        h_ref[...] = _rmsnorm_rows(x_ref[...], g_ref[...]).astype(BF16)

    o_ref[...] = jnp.dot(h_ref[...], w_ref[...], preferred_element_type=F32)


def _norm_matmul(x, g, w, *, tm, tn):
    n, d = x.shape
    nout = w.shape[1]
    return pl.pallas_call(
        _norm_matmul_kernel,
        grid=(n // tm, nout // tn),
        in_specs=[
            pl.BlockSpec((tm, d), lambda i, j: (i, 0)),
            pl.BlockSpec((1, d), lambda i, j: (0, 0)),
            pl.BlockSpec((d, tn), lambda i, j: (0, j)),
        ],
        out_specs=pl.BlockSpec((tm, tn), lambda i, j: (i, j)),
        out_shape=jax.ShapeDtypeStruct((n, nout), F32),
        scratch_shapes=[pltpu.VMEM((tm, d), BF16)],
        compiler_params=_params("parallel", "arbitrary"),
        name="norm_matmul",
    )(x, g.reshape(1, d), w)


def _sb_kernel(q_ref, k_ref, v_ref, o_ref, run_ref, *, tq, tk, q_offset):
    qi = pl.program_id(1)
    q_pos0 = q_offset + qi * tq
    n_kb = jnp.minimum(pl.cdiv(q_pos0 + tq, tk), k_ref.shape[1] // tk)

    run_ref[...] = jnp.zeros_like(run_ref)
    o_ref[...] = jnp.zeros_like(o_ref)

    row = lax.broadcasted_iota(jnp.int32, (tq, tk), 0)
    col = lax.broadcasted_iota(jnp.int32, (tk, 2 * tk), 1)
    krow = lax.broadcasted_iota(jnp.int32, (tk, 2 * tk), 0)
    suffix_and_total = jnp.where((krow >= col) | (col >= tk), 1.0, 0.0).astype(BF16)
    col_minus_row = lax.broadcasted_iota(jnp.int32, (tq, tk), 1) - row

    def body(it, carry):
        kb = n_kb - 1 - it
        k0 = pl.multiple_of(kb * tk, tk)
        mask = col_minus_row < (q_pos0 - k0)
        for h in range(N_HEADS):
            hs = slice(h * HEAD_DIM, (h + 1) * HEAD_DIM)
            qh = (q_ref[0, :, hs] * (HEAD_DIM ** -0.5)).astype(BF16)
            kh = k_ref[0, pl.ds(k0, tk), hs].astype(BF16)
            vh = v_ref[0, pl.ds(k0, tk), hs].astype(BF16)
            z = lax.dot_general(qh, kh, (((1,), (1,)), ((), ())), preferred_element_type=F32)
            lf = -(jnp.maximum(z, 0.0) + jnp.log1p(jnp.exp(-jnp.abs(z))))
            lf = jnp.where(mask, lf, 0.0)
            lf_hi = lf.astype(BF16)
            lf_lo = (lf - lf_hi.astype(F32)).astype(BF16)
            sums = (jnp.dot(lf_hi, suffix_and_total, preferred_element_type=F32)
                    + jnp.dot(lf_lo, suffix_and_total, preferred_element_type=F32))
            run = run_ref[h]
            w = jnp.where(mask, jnp.exp(z + sums[:, :tk] + run), 0.0)
            o_ref[0, :, hs] += jnp.dot(w.astype(BF16), vh, preferred_element_type=F32)
            run_ref[h] = run + sums[:, tk:]
        return carry

    lax.fori_loop(0, n_kb, body, 0)


def _sb_attention(q_arr, q_col, k_arr, k_col, v_arr, v_col, *, t_q, tq, tk, q_offset):
    b = q_arr.shape[0]
    t_k = k_arr.shape[1]
    kern = functools.partial(_sb_kernel, tq=tq, tk=tk, q_offset=q_offset)
    return pl.pallas_call(
        kern,
        grid=(b, t_q // tq),
        in_specs=[
            pl.BlockSpec((1, tq, G_WIDTH), lambda i, j: (i, j, q_col)),
            pl.BlockSpec((1, t_k, G_WIDTH), lambda i, j: (i, 0, k_col)),
            pl.BlockSpec((1, t_k, G_WIDTH), lambda i, j: (i, 0, v_col)),
        ],
        out_specs=pl.BlockSpec((1, tq, G_WIDTH), lambda i, j: (i, j, 0)),
        out_shape=jax.ShapeDtypeStruct((b, t_q, G_WIDTH), F32),
        scratch_shapes=[pltpu.VMEM((N_HEADS, tq, tk), F32)],
        compiler_params=_params("parallel", "arbitrary"),
        name="sb_attention",
    )(q_arr, k_arr, v_arr)


SC_HIST = 8
CC_HIST = 32


def _mixers_kernel(scb_ref, scc_ref, sch_ref, cfa_ref, cfg_ref, gmu_ref, gmv_ref,
                   sc_past_ref, cc_past_ref, wsc_ref, wcc_ref, bcc_ref, gcn_ref, bcn_ref,
                   gvn_ref, bvn_ref, wsp_ref, bsp_ref,
                   y_ref, vn_ref, sc_state_ref, cc_state_ref, u_buf, g_buf, *, tt, chunk):
    ti = pl.program_id(1)

    @pl.when(ti == 0)
    def _():
        u_buf[0:SC_HIST, :] = jnp.zeros((SC_HIST, G_WIDTH), F32)
        g_buf[0:CC_HIST, :] = jnp.zeros((CC_HIST, G_WIDTH), F32)
        u_buf[SC_HIST - (SCONV_W - 1):SC_HIST, :] = sc_past_ref[0]
        g_buf[CC_HIST - (CCONV_W - 1):CC_HIST, :] = cc_past_ref[0]

    u_buf[SC_HIST:SC_HIST + tt, :] = scc_ref[0] * sch_ref[0]
    conv_b = jnp.zeros((tt, G_WIDTH), F32)
    for k in range(SCONV_W):
        s0 = SC_HIST - (SCONV_W - 1) + k
        conv_b = conv_b + wsc_ref[k:k + 1, :] * u_buf[s0:s0 + tt, :]
    y_ref[0, :, 0:G_WIDTH] = scb_ref[0] * conv_b

    g_buf[CC_HIST:CC_HIST + tt, :] = cfa_ref[0] * jax.nn.sigmoid(cfg_ref[0])
    conv_c = jnp.zeros((tt, G_WIDTH), F32)
    for k in range(CCONV_W):
        s0 = CC_HIST - (CCONV_W - 1) + k
        conv_c = conv_c + wcc_ref[k:k + 1, :] * g_buf[s0:s0 + tt, :]
    ln = _layernorm(conv_c + bcc_ref[...], gcn_ref[...], bcn_ref[...])
    y_ref[0, :, G_WIDTH:2 * G_WIDTH] = ln * jax.nn.sigmoid(ln)

    vn = _layernorm(_gelu_tanh(gmv_ref[0]), gvn_ref[...], bvn_ref[...])
    vn_ref[0] = vn
    vn16 = vn.astype(BF16)
    for c in range(tt // chunk):
        rows = slice(c * chunk, (c + 1) * chunk)
        parts = []
        for h in range(N_HEADS):
            hs = slice(h * HEAD_DIM, (h + 1) * HEAD_DIM)
            parts.append(jnp.dot(wsp_ref[h], vn16[rows, hs], preferred_element_type=F32))
        s = jnp.concatenate(parts, axis=-1) + bsp_ref[...]
        y_ref[0, rows, 2 * G_WIDTH:3 * G_WIDTH] = _gelu_tanh(gmu_ref[0, rows, :]) * s

    sc_state_ref[0] = u_buf[SC_HIST + tt - (SCONV_W - 1):SC_HIST + tt, :]
    cc_state_ref[0] = g_buf[CC_HIST + tt - (CCONV_W - 1):CC_HIST + tt, :]
    u_buf[0:SC_HIST, :] = u_buf[tt:tt + SC_HIST, :]
    g_buf[0:CC_HIST, :] = g_buf[tt:tt + CC_HIST, :]


def _mixers(proj3, sc_past, cc_past, w_sconv, w_cconv, b_cconv, g_cnorm, b_cnorm, g_vnorm, b_vnorm,
            w_spatial, b_spatial, *, tt):
    b, t, _ = proj3.shape
    chunk = min(t, MLP_CHUNK)
    wsp = jnp.tril(w_spatial[:, :chunk, :chunk]).astype(BF16)
    bsp = jnp.repeat(b_spatial[:, :chunk].T, HEAD_DIM, axis=1)

    def col(c):
        return pl.BlockSpec((1, tt, G_WIDTH), lambda i, j: (i, j, c))

    def whole(a):
        return pl.BlockSpec(a.shape, lambda i, j: (0,) * a.ndim)

    def per_batch(a):
        return pl.BlockSpec((1,) + a.shape[1:], lambda i, j: (i,) + (0,) * (a.ndim - 1))

    row = lambda a: a.reshape(1, G_WIDTH)
    small = [w_sconv, w_cconv, row(b_cconv), row(g_cnorm), row(b_cnorm), row(g_vnorm), row(b_vnorm), wsp, bsp]
    kern = functools.partial(_mixers_kernel, tt=tt, chunk=chunk)
    return pl.pallas_call(
        kern,
        grid=(b, t // tt),
        in_specs=[col(c) for c in range(3, 10)] + [per_batch(sc_past), per_batch(cc_past)]
                 + [whole(a) for a in small],
        out_specs=[
            pl.BlockSpec((1, tt, 3 * G_WIDTH), lambda i, j: (i, j, 0)),
            pl.BlockSpec((1, tt, G_WIDTH), lambda i, j: (i, j, 0)),
            pl.BlockSpec((1, SCONV_W - 1, G_WIDTH), lambda i, j: (i, 0, 0)),
            pl.BlockSpec((1, CCONV_W - 1, G_WIDTH), lambda i, j: (i, 0, 0)),
        ],
        out_shape=[
            jax.ShapeDtypeStruct((b, t, 3 * G_WIDTH), F32),
            jax.ShapeDtypeStruct((b, t, G_WIDTH), F32),
            jax.ShapeDtypeStruct((b, SCONV_W - 1, G_WIDTH), F32),
            jax.ShapeDtypeStruct((b, CCONV_W - 1, G_WIDTH), F32),
        ],
        scratch_shapes=[pltpu.VMEM((SC_HIST + tt, G_WIDTH), F32), pltpu.VMEM((CC_HIST + tt, G_WIDTH), F32)],
        compiler_params=_params("parallel", "arbitrary"),
        name="mixers",
    )(*([proj3] * 7), sc_past, cc_past, *small)


def _out_proj_kernel(x_ref, ya_ref, yb_ref, wa_ref, wb_ref, o_ref):
    o_ref[...] = (x_ref[...]
                  + jnp.dot(ya_ref[...].astype(BF16), wa_ref[...], preferred_element_type=F32)
                  + jnp.dot(yb_ref[...].astype(BF16), wb_ref[...], preferred_element_type=F32))


def _out_proj(x, ya, yb, w_out, *, tm):
    n, d = x.shape
    wa = w_out[:G_WIDTH].astype(BF16)
    wb = w_out[G_WIDTH:].astype(BF16)
    return pl.pallas_call(
        _out_proj_kernel,
        grid=(n // tm,),
        in_specs=[
            pl.BlockSpec((tm, d), lambda i: (i, 0)),
            pl.BlockSpec((tm, G_WIDTH), lambda i: (i, 0)),
            pl.BlockSpec((tm, 3 * G_WIDTH), lambda i: (i, 0)),
            pl.BlockSpec(wa.shape, lambda i: (0, 0)),
            pl.BlockSpec(wb.shape, lambda i: (0, 0)),
        ],
        out_specs=pl.BlockSpec((tm, d), lambda i: (i, 0)),
        out_shape=jax.ShapeDtypeStruct((n, d), F32),
        compiler_params=_params("parallel"),
        name="out_proj",
    )(x, ya, yb, wa, wb)


def _top16_rows(s, n_rows):
    rid = lax.broadcasted_iota(jnp.int32, s.shape, 0).astype(F32)
    vals, idxs = [], []
    for _ in range(PEER_TOPK):
        m = jnp.max(s, axis=0, keepdims=True)
        i = jnp.min(jnp.where(s == m, rid, float(n_rows)), axis=0, keepdims=True)
        s = jnp.where(rid == i, -jnp.inf, s)
        vals.append(m)
        idxs.append(i)
    return jnp.concatenate(vals, axis=0), jnp.concatenate(idxs, axis=0)


def _peer_topk_kernel(q_ref, k1_ref, k2_ref, eidx_ref, gate_ref):
    nt = (((1,), (1,)), ((), ()))
    e_rows, g_rows = [], []
    for h in range(PEER_HEADS):
        qa = q_ref[:, (2 * h) * PEER_HALF:(2 * h + 1) * PEER_HALF].astype(BF16)
        qb = q_ref[:, (2 * h + 1) * PEER_HALF:(2 * h + 2) * PEER_HALF].astype(BF16)
        s1 = lax.dot_general(k1_ref[h].astype(BF16), qa, nt, preferred_element_type=F32)
        s2 = lax.dot_general(k2_ref[h].astype(BF16), qb, nt, preferred_element_type=F32)
        t1, i1 = _top16_rows(s1, PEER_KEYS)
        t2, i2 = _top16_rows(s2, PEER_KEYS)
        cand = jnp.concatenate([t1[a:a + 1] + t2 for a in range(PEER_TOPK)], axis=0)
        cidx = jnp.concatenate([i1[a:a + 1] * float(PEER_KEYS) + i2 for a in range(PEER_TOPK)], axis=0)
        top, sel = _top16_rows(cand, PEER_TOPK * PEER_TOPK)
        rid = lax.broadcasted_iota(jnp.int32, cand.shape, 0).astype(F32)
        eidx = [jnp.sum(jnp.where(rid == sel[r:r + 1], cidx, 0.0), axis=0, keepdims=True)
                for r in range(PEER_TOPK)]
        p = jnp.exp(top - top[0:1])
        g_rows.append(p / jnp.sum(p, axis=0, keepdims=True))
        e_rows.append(jnp.concatenate(eidx, axis=0))
    eidx_ref[...] = jnp.concatenate(e_rows, axis=0).T.astype(jnp.int32)
    gate_ref[...] = jnp.concatenate(g_rows, axis=0).T


def _peer_topk(q, keys1, keys2, *, tn):
    n, d = q.shape
    return pl.pallas_call(
        _peer_topk_kernel,
        grid=(n // tn,),
        in_specs=[
            pl.BlockSpec((tn, d), lambda i: (i, 0)),
            pl.BlockSpec(keys1.shape, lambda i: (0, 0, 0)),
            pl.BlockSpec(keys2.shape, lambda i: (0, 0, 0)),
        ],
        out_specs=[pl.BlockSpec((tn, N_SEL), lambda i: (i, 0)), pl.BlockSpec((tn, N_SEL), lambda i: (i, 0))],
        out_shape=[jax.ShapeDtypeStruct((n, N_SEL), jnp.int32), jax.ShapeDtypeStruct((n, N_SEL), F32)],
        compiler_params=_params("parallel"),
        name="peer_topk",
    )(q, keys1, keys2)


N_SLOTS = 3
HI16 = -65536


def _pack_experts_kernel(u_ref, v_ref, o_ref):
    ub = lax.bitcast_convert_type(u_ref[...].astype(BF16).astype(F32), jnp.int32)
    vb = lax.bitcast_convert_type(v_ref[...].astype(BF16).astype(F32), jnp.int32)
    o_ref[...] = lax.shift_right_logical(ub, 16) | (vb & HI16)


def _pack_experts(exp_u, exp_v, layer, *, tr):
    _, n, d = exp_u.shape
    spec = pl.BlockSpec((None, tr, d), lambda i: (layer, i, 0))
    return pl.pallas_call(
        _pack_experts_kernel,
        grid=(n // tr,),
        in_specs=[spec, spec],
        out_specs=pl.BlockSpec((tr, d), lambda i: (i, 0)),
        out_shape=jax.ShapeDtypeStruct((n, d), jnp.int32),
        compiler_params=_params("parallel"),
        name="pack_experts",
    )(exp_u, exp_v)


def _peer_experts_kernel(eidx_ref, x_ref, gate_ref, g_ref, gf_ref, uv_hbm, o_ref, buf, sem, *, tb, final_norm):
    def issue(t, slot):
        for m in range(N_SEL):
            e = eidx_ref[t, m]
            pltpu.make_async_copy(uv_hbm.at[pl.ds(e, 1), :], buf.at[slot, pl.ds(m, 1), :],
                                  sem.at[slot]).start(priority=m % 2)

    def wait(slot):
        pltpu.make_async_copy(uv_hbm.at[pl.ds(0, N_SEL), :], buf.at[slot], sem.at[slot]).wait()

    eye = (lax.broadcasted_iota(jnp.int32, (N_SEL, N_SEL), 0)
           == lax.broadcasted_iota(jnp.int32, (N_SEL, N_SEL), 1))

    def compute(t, slot):
        x = x_ref[pl.ds(t, 1), :]
        h = _rmsnorm_rows(x, g_ref[...])
        words = buf[slot]
        u = lax.bitcast_convert_type(lax.shift_left(words, 16), F32)
        v = lax.bitcast_convert_type(words & HI16, F32)
        act = _gelu_tanh(jnp.sum(u * h, axis=-1, keepdims=True))
        gate_col = jnp.sum(jnp.where(eye, gate_ref[pl.ds(t, 1), :], 0.0), axis=-1, keepdims=True)
        y = x + jnp.sum((gate_col * act) * v, axis=0, keepdims=True)
        if final_norm:
            y = _rmsnorm_rows(y, gf_ref[...])
        o_ref[pl.ds(t, 1), :] = y

    for t in range(N_SLOTS - 1):
        issue(t, t)

    def body(t, carry):
        slot = lax.rem(t, N_SLOTS)
        wait(slot)

        @pl.when(t + (N_SLOTS - 1) < tb)
        def _():
            issue(t + (N_SLOTS - 1), lax.rem(t + (N_SLOTS - 1), N_SLOTS))

        compute(t, slot)
        return carry

    lax.fori_loop(0, tb, body, 0)


def _peer_experts(x, eidx, gate, g_ffn, g_final, exp_uv, *, tb, final_norm):
    n, d = x.shape
    kern = functools.partial(_peer_experts_kernel, tb=tb, final_norm=final_norm)
    return pl.pallas_call(
        kern,
        grid=(n // tb,),
        in_specs=[
            pl.BlockSpec((tb, N_SEL), lambda i: (i, 0), memory_space=pltpu.SMEM),
            pl.BlockSpec((tb, d), lambda i: (i, 0)),
            pl.BlockSpec((tb, N_SEL), lambda i: (i, 0)),
            pl.BlockSpec((1, d), lambda i: (0, 0)),
            pl.BlockSpec((1, d), lambda i: (0, 0)),
            pl.BlockSpec(memory_space=pl.ANY),
        ],
        out_specs=pl.BlockSpec((tb, d), lambda i: (i, 0)),
        out_shape=jax.ShapeDtypeStruct((n, d), F32),
        scratch_shapes=[
            pltpu.VMEM((N_SLOTS, N_SEL, d), jnp.int32),
            pltpu.SemaphoreType.DMA((N_SLOTS,)),
        ],
        compiler_params=_params("arbitrary"),
        name="peer_experts",
    )(eidx, x, gate, g_ffn.reshape(1, d), g_final.reshape(1, d), exp_uv)


def _layer(x3, k_past, v_past, sc_past, cc_past, w, g_final, final_norm):
    (g_mix, w_in16, w_sconv, w_cconv, b_cconv, g_cnorm, b_cnorm, g_vnorm, b_vnorm,
     w_spatial, b_spatial, w_out, g_ffn, w_query16, keys1, keys2, exp_uv) = w
    b, t, d = x3.shape
    n = b * t
    x = x3.reshape(n, d)
    tm = min(512, n)

    proj = _norm_matmul(x, g_mix, w_in16, tm=tm, tn=1280)
    proj3 = proj.reshape(b, t, proj.shape[1])
    k_new = proj3[:, :, G_WIDTH:2 * G_WIDTH]
    v_new = proj3[:, :, 2 * G_WIDTH:3 * G_WIDTH]
    if k_past is None:
        y_a = _sb_attention(proj3, 0, proj3, 1, proj3, 2, t_q=t, tq=128, tk=128, q_offset=0)
    else:
        past = k_past.shape[1]
        k_all = jnp.concatenate([k_past.reshape(b, past, G_WIDTH), k_new], axis=1)
        v_all = jnp.concatenate([v_past.reshape(b, past, G_WIDTH), v_new], axis=1)
        y_a = _sb_attention(proj3, 0, k_all, 0, v_all, 0, t_q=t, tq=t, tk=t, q_offset=past)
    y_bcd, vn, sc_state, cc_state = _mixers(
        proj3, sc_past, cc_past, w_sconv, w_cconv, b_cconv, g_cnorm, b_cnorm, g_vnorm, b_vnorm,
        w_spatial, b_spatial, tt=min(512, t))
    x1 = _out_proj(x, y_a.reshape(n, G_WIDTH), y_bcd.reshape(n, 3 * G_WIDTH), w_out, tm=min(256, n))
    q = _norm_matmul(x1, g_ffn, w_query16, tm=tm, tn=1024)
    eidx, gate = _peer_topk(q, keys1, keys2, tn=min(256, n))
    x2 = _peer_experts(x1, eidx, gate, g_ffn, g_final, exp_uv, tb=min(64, n), final_norm=final_norm)
    heads = (b, t, N_HEADS, HEAD_DIM)
    return x2.reshape(b, t, d), (k_new.reshape(heads), v_new.reshape(heads), sc_state, cc_state, vn)


def kernel(x_prompt, x_sample, cache_k, cache_v, state_sconv, state_cconv, g_mix, w_in, w_sconv, w_cconv,
           b_cconv, g_cnorm, b_cnorm, g_vnorm, b_vnorm, w_spatial, b_spatial, w_out, g_ffn, w_query,
           sub_keys1, sub_keys2, expert_u, expert_v, g_final):
    depth = w_in.shape[0]
    xp, xs = x_prompt, x_sample
    nb = xp.shape[0]
    outs = [[] for _ in range(9)]
    for l in range(depth):
        w = (g_mix[l], w_in[l].astype(BF16), w_sconv[l], w_cconv[l], b_cconv[l], g_cnorm[l], b_cnorm[l],
             g_vnorm[l], b_vnorm[l], w_spatial[l], b_spatial[l], w_out[l], g_ffn[l],
             w_query[l].astype(BF16), sub_keys1[l], sub_keys2[l],
             _pack_experts(expert_u, expert_v, l, tr=256))
        last = l == depth - 1
        sc0 = jnp.zeros((nb, SCONV_W - 1, G_WIDTH), F32)
        cc0 = jnp.zeros((nb, CCONV_W - 1, G_WIDTH), F32)
        xp, (k1, v1, sc1, cc1, _) = _layer(xp, None, None, sc0, cc0, w, g_final, last)
        xs, (k2, v2, sc2, cc2, gv2) = _layer(xs, cache_k[l], cache_v[l], state_sconv[l], state_cconv[l],
                                             w, g_final, last)
        for lst, val in zip(outs, (k1, v1, k2, v2, sc1, sc2, cc1, cc2, gv2)):
            lst.append(val)
    kp, vp, ksm, vsm, scp, scs, ccp, ccs, gvs = [jnp.stack(o) for o in outs]
    return (xp, xs, kp, vp, ksm, vsm, scp, scs, ccp, ccs, gvs)
```

```python
import functools

import jax
import jax.numpy as jnp
from jax import lax
from jax.experimental import pallas as pl
from jax.experimental.pallas import tpu as pltpu

F32 = jnp.float32
BF16 = jnp.bfloat16
EPS = 1e-6

G_WIDTH = 512
N_HEADS = 8
HEAD_DIM = 64
SCONV_W = 3
CCONV_W = 31
MLP_CHUNK = 128
PEER_HEADS = 8
PEER_KEYS = 128
PEER_TOPK = 16
PEER_HALF = 128
N_SEL = PEER_HEADS * PEER_TOPK

VMEM_LIMIT_BYTES = 56 * 1024 * 1024


def _params(*sem):
    return pltpu.CompilerParams(dimension_semantics=sem, vmem_limit_bytes=VMEM_LIMIT_BYTES)


def _gelu_tanh(x):
    return 0.5 * x * (1.0 + jnp.tanh(0.7978845608028654 * (x + 0.044715 * (x * x * x))))


def _layernorm(x, g, b):
    xc = x - jnp.mean(x, axis=-1, keepdims=True)
    var = jnp.mean(xc * xc, axis=-1, keepdims=True)
    return xc * lax.rsqrt(var + EPS) * g + b


def _rmsnorm_rows(x, g):
    return x * lax.rsqrt(jnp.mean(x * x, axis=-1, keepdims=True) + EPS) * g


def _norm_matmul_kernel(x_ref, g_ref, w_ref, o_ref, h_ref):
    @pl.when(pl.program_id(1) == 0)
    def _():
        h_ref[...] = _rmsnorm_rows(x_ref[...], g_ref[...]).astype(BF16)

    o_ref[...] = jnp.dot(h_ref[...], w_ref[...], preferred_element_type=F32)


def _norm_matmul(x, g, w, *, tm, tn):
    n, d = x.shape
    nout = w.shape[1]
    return pl.pallas_call(
        _norm_matmul_kernel,
        grid=(n // tm, nout // tn),
        in_specs=[
            pl.BlockSpec((tm, d), lambda i, j: (i, 0)),
            pl.BlockSpec((1, d), lambda i, j: (0, 0)),
            pl.BlockSpec((d, tn), lambda i, j: (0, j)),
        ],
        out_specs=pl.BlockSpec((tm, tn), lambda i, j: (i, j)),
        out_shape=jax.ShapeDtypeStruct((n, nout), F32),
        scratch_shapes=[pltpu.VMEM((tm, d), BF16)],
        compiler_params=_params("parallel", "arbitrary"),
        name="norm_matmul",
    )(x, g.reshape(1, d), w)


RUN_CUTOFF = -104.0


def _sb_kernel(q_ref, k_ref, v_ref, o_ref, run_ref, *, tq, tk, q_offset):
    qi = pl.program_id(1)
    q_pos0 = q_offset + qi * tq
    n_kb = jnp.minimum(pl.cdiv(q_pos0 + tq, tk), k_ref.shape[1] // tk)

    run_ref[...] = jnp.zeros_like(run_ref)
    o_ref[...] = jnp.zeros_like(o_ref)

    row = lax.broadcasted_iota(jnp.int32, (tq, tk), 0)
    col = lax.broadcasted_iota(jnp.int32, (tk, 2 * tk), 1)
    krow = lax.broadcasted_iota(jnp.int32, (tk, 2 * tk), 0)
    suffix_and_total = jnp.where((krow >= col) | (col >= tk), 1.0, 0.0).astype(BF16)
    col_minus_row = lax.broadcasted_iota(jnp.int32, (tq, tk), 1) - row

    def body(carry):
        it, _ = carry
        kb = n_kb - 1 - it
        k0 = pl.multiple_of(kb * tk, tk)
        mask = col_minus_row < (q_pos0 - k0)
        for h in range(N_HEADS):
            hs = slice(h * HEAD_DIM, (h + 1) * HEAD_DIM)
            qh = (q_ref[0, :, hs] * (HEAD_DIM ** -0.5)).astype(BF16)
            kh = k_ref[0, pl.ds(k0, tk), hs].astype(BF16)
            vh = v_ref[0, pl.ds(k0, tk), hs].astype(BF16)
            z = lax.dot_general(qh, kh, (((1,), (1,)), ((), ())), preferred_element_type=F32)
            lf = -(jnp.maximum(z, 0.0) + jnp.log1p(jnp.exp(-jnp.abs(z))))
            lf = jnp.where(mask, lf, 0.0)
            lf_hi = lf.astype(BF16)
            lf_lo = (lf - lf_hi.astype(F32)).astype(BF16)
            sums = (jnp.dot(lf_hi, suffix_and_total, preferred_element_type=F32)
                    + jnp.dot(lf_lo, suffix_and_total, preferred_element_type=F32))
            run = run_ref[h]
            w = jnp.where(mask, jnp.exp(z + sums[:, :tk] + run), 0.0)
            o_ref[0, :, hs] += jnp.dot(w.astype(BF16), vh, preferred_element_type=F32)
            run_ref[h] = run + sums[:, tk:]
        return it + 1, jnp.max(run_ref[...])

    lax.while_loop(lambda c: (c[0] < n_kb) & (c[1] > RUN_CUTOFF), body, (0, 0.0))


def _sb_attention(q_arr, q_col, k_arr, k_col, v_arr, v_col, *, t_q, tq, tk, q_offset):
    b = q_arr.shape[0]
    t_k = k_arr.shape[1]
    kern = functools.partial(_sb_kernel, tq=tq, tk=tk, q_offset=q_offset)
    return pl.pallas_call(
        kern,
        grid=(b, t_q // tq),
        in_specs=[
            pl.BlockSpec((1, tq, G_WIDTH), lambda i, j: (i, j, q_col)),
            pl.BlockSpec((1, t_k, G_WIDTH), lambda i, j: (i, 0, k_col)),
            pl.BlockSpec((1, t_k, G_WIDTH), lambda i, j: (i, 0, v_col)),
        ],
        out_specs=pl.BlockSpec((1, tq, G_WIDTH), lambda i, j: (i, j, 0)),
        out_shape=jax.ShapeDtypeStruct((b, t_q, G_WIDTH), F32),
        scratch_shapes=[pltpu.VMEM((N_HEADS, tq, tk), F32)],
        compiler_params=_params("parallel", "arbitrary"),
        name="sb_attention",
    )(q_arr, k_arr, v_arr)


SC_HIST = 8
CC_HIST = 32


def _mixers_kernel(scb_ref, scc_ref, sch_ref, cfa_ref, cfg_ref, gmu_ref, gmv_ref,
                   sc_past_ref, cc_past_ref, wsc_ref, wcc_ref, bcc_ref, gcn_ref, bcn_ref,
                   gvn_ref, bvn_ref, wsp_ref, bsp_ref,
                   y_ref, vn_ref, sc_state_ref, cc_state_ref, u_buf, g_buf, *, tt, chunk):
    ti = pl.program_id(1)

    @pl.when(ti == 0)
    def _():
        u_buf[0:SC_HIST, :] = jnp.zeros((SC_HIST, G_WIDTH), F32)
        g_buf[0:CC_HIST, :] = jnp.zeros((CC_HIST, G_WIDTH), F32)
        u_buf[SC_HIST - (SCONV_W - 1):SC_HIST, :] = sc_past_ref[0]
        g_buf[CC_HIST - (CCONV_W - 1):CC_HIST, :] = cc_past_ref[0]

    u_buf[SC_HIST:SC_HIST + tt, :] = scc_ref[0] * sch_ref[0]
    conv_b = jnp.zeros((tt, G_WIDTH), F32)
    for k in range(SCONV_W):
        s0 = SC_HIST - (SCONV_W - 1) + k
        conv_b = conv_b + wsc_ref[k:k + 1, :] * u_buf[s0:s0 + tt, :]
    y_ref[0, :, 0:G_WIDTH] = scb_ref[0] * conv_b

    g_buf[CC_HIST:CC_HIST + tt, :] = cfa_ref[0] * jax.nn.sigmoid(cfg_ref[0])
    conv_c = jnp.zeros((tt, G_WIDTH), F32)
    for k in range(CCONV_W):
        s0 = CC_HIST - (CCONV_W - 1) + k
        conv_c = conv_c + wcc_ref[k:k + 1, :] * g_buf[s0:s0 + tt, :]
    ln = _layernorm(conv_c + bcc_ref[...], gcn_ref[...], bcn_ref[...])
    y_ref[0, :, G_WIDTH:2 * G_WIDTH] = ln * jax.nn.sigmoid(ln)

    vn = _layernorm(_gelu_tanh(gmv_ref[0]), gvn_ref[...], bvn_ref[...])
    vn_ref[0] = vn
    vn16 = vn.astype(BF16)
    for c in range(tt // chunk):
        rows = slice(c * chunk, (c + 1) * chunk)
        parts = []
        for h in range(N_HEADS):
            hs = slice(h * HEAD_DIM, (h + 1) * HEAD_DIM)
            parts.append(jnp.dot(wsp_ref[h], vn16[rows, hs], preferred_element_type=F32))
        s = jnp.concatenate(parts, axis=-1) + bsp_ref[...]
        y_ref[0, rows, 2 * G_WIDTH:3 * G_WIDTH] = _gelu_tanh(gmu_ref[0, rows, :]) * s

    sc_state_ref[0] = u_buf[SC_HIST + tt - (SCONV_W - 1):SC_HIST + tt, :]
    cc_state_ref[0] = g_buf[CC_HIST + tt - (CCONV_W - 1):CC_HIST + tt, :]
    u_buf[0:SC_HIST, :] = u_buf[tt:tt + SC_HIST, :]
    g_buf[0:CC_HIST, :] = g_buf[tt:tt + CC_HIST, :]


def _mixers(proj3, sc_past, cc_past, w_sconv, w_cconv, b_cconv, g_cnorm, b_cnorm, g_vnorm, b_vnorm,
            w_spatial, b_spatial, *, tt):
    b, t, _ = proj3.shape
    chunk = min(t, MLP_CHUNK)
    wsp = jnp.tril(w_spatial[:, :chunk, :chunk]).astype(BF16)
    bsp = jnp.repeat(b_spatial[:, :chunk].T, HEAD_DIM, axis=1)

    def col(c):
        return pl.BlockSpec((1, tt, G_WIDTH), lambda i, j: (i, j, c))

    def whole(a):
        return pl.BlockSpec(a.shape, lambda i, j: (0,) * a.ndim)

    def per_batch(a):
        return pl.BlockSpec((1,) + a.shape[1:], lambda i, j: (i,) + (0,) * (a.ndim - 1))

    row = lambda a: a.reshape(1, G_WIDTH)
    small = [w_sconv, w_cconv, row(b_cconv), row(g_cnorm), row(b_cnorm), row(g_vnorm), row(b_vnorm), wsp, bsp]
    kern = functools.partial(_mixers_kernel, tt=tt, chunk=chunk)
    return pl.pallas_call(
        kern,
        grid=(b, t // tt),
        in_specs=[col(c) for c in range(3, 10)] + [per_batch(sc_past), per_batch(cc_past)]
                 + [whole(a) for a in small],
        out_specs=[
            pl.BlockSpec((1, tt, 3 * G_WIDTH), lambda i, j: (i, j, 0)),
            pl.BlockSpec((1, tt, G_WIDTH), lambda i, j: (i, j, 0)),
            pl.BlockSpec((1, SCONV_W - 1, G_WIDTH), lambda i, j: (i, 0, 0)),
            pl.BlockSpec((1, CCONV_W - 1, G_WIDTH), lambda i, j: (i, 0, 0)),
        ],
        out_shape=[
            jax.ShapeDtypeStruct((b, t, 3 * G_WIDTH), F32),
            jax.ShapeDtypeStruct((b, t, G_WIDTH), F32),
            jax.ShapeDtypeStruct((b, SCONV_W - 1, G_WIDTH), F32),
            jax.ShapeDtypeStruct((b, CCONV_W - 1, G_WIDTH), F32),
        ],
        scratch_shapes=[pltpu.VMEM((SC_HIST + tt, G_WIDTH), F32), pltpu.VMEM((CC_HIST + tt, G_WIDTH), F32)],
        compiler_params=_params("parallel", "arbitrary"),
        name="mixers",
    )(*([proj3] * 7), sc_past, cc_past, *small)


def _out_proj_kernel(x_ref, ya_ref, yb_ref, wa_ref, wb_ref, o_ref):
    o_ref[...] = (x_ref[...]
                  + jnp.dot(ya_ref[...].astype(BF16), wa_ref[...], preferred_element_type=F32)
                  + jnp.dot(yb_ref[...].astype(BF16), wb_ref[...], preferred_element_type=F32))


def _out_proj(x, ya, yb, w_out, *, tm):
    n, d = x.shape
    wa = w_out[:G_WIDTH].astype(BF16)
    wb = w_out[G_WIDTH:].astype(BF16)
    return pl.pallas_call(
        _out_proj_kernel,
        grid=(n // tm,),
        in_specs=[
            pl.BlockSpec((tm, d), lambda i: (i, 0)),
            pl.BlockSpec((tm, G_WIDTH), lambda i: (i, 0)),
            pl.BlockSpec((tm, 3 * G_WIDTH), lambda i: (i, 0)),
            pl.BlockSpec(wa.shape, lambda i: (0, 0)),
            pl.BlockSpec(wb.shape, lambda i: (0, 0)),
        ],
        out_specs=pl.BlockSpec((tm, d), lambda i: (i, 0)),
        out_shape=jax.ShapeDtypeStruct((n, d), F32),
        compiler_params=_params("parallel"),
        name="out_proj",
    )(x, ya, yb, wa, wb)


def _top16_rows(s, n_rows):
    rid = lax.broadcasted_iota(jnp.int32, s.shape, 0).astype(F32)
    vals, idxs = [], []
    for _ in range(PEER_TOPK):
        m = jnp.max(s, axis=0, keepdims=True)
        i = jnp.min(jnp.where(s == m, rid, float(n_rows)), axis=0, keepdims=True)
        s = jnp.where(rid == i, -jnp.inf, s)
        vals.append(m)
        idxs.append(i)
    return jnp.concatenate(vals, axis=0), jnp.concatenate(idxs, axis=0)


def _peer_topk_kernel(q_ref, k1_ref, k2_ref, eidx_ref, gate_ref):
    nt = (((1,), (1,)), ((), ()))
    e_rows, g_rows = [], []
    for h in range(PEER_HEADS):
        qa = q_ref[:, (2 * h) * PEER_HALF:(2 * h + 1) * PEER_HALF].astype(BF16)
        qb = q_ref[:, (2 * h + 1) * PEER_HALF:(2 * h + 2) * PEER_HALF].astype(BF16)
        s1 = lax.dot_general(k1_ref[h].astype(BF16), qa, nt, preferred_element_type=F32)
        s2 = lax.dot_general(k2_ref[h].astype(BF16), qb, nt, preferred_element_type=F32)
        t1, i1 = _top16_rows(s1, PEER_KEYS)
        t2, i2 = _top16_rows(s2, PEER_KEYS)
        cand = jnp.concatenate([t1[a:a + 1] + t2 for a in range(PEER_TOPK)], axis=0)
        cidx = jnp.concatenate([i1[a:a + 1] * float(PEER_KEYS) + i2 for a in range(PEER_TOPK)], axis=0)
        top, sel = _top16_rows(cand, PEER_TOPK * PEER_TOPK)
        rid = lax.broadcasted_iota(jnp.int32, cand.shape, 0).astype(F32)
        eidx = [jnp.sum(jnp.where(rid == sel[r:r + 1], cidx, 0.0), axis=0, keepdims=True)
                for r in range(PEER_TOPK)]
        p = jnp.exp(top - top[0:1])
        g_rows.append(p / jnp.sum(p, axis=0, keepdims=True))
        e_rows.append(jnp.concatenate(eidx, axis=0))
    eidx_ref[...] = jnp.concatenate(e_rows, axis=0).T.astype(jnp.int32)
    gate_ref[...] = jnp.concatenate(g_rows, axis=0).T


def _peer_topk(q, keys1, keys2, *, tn):
    n, d = q.shape
    return pl.pallas_call(
        _peer_topk_kernel,
        grid=(n // tn,),
        in_specs=[
            pl.BlockSpec((tn, d), lambda i: (i, 0)),
            pl.BlockSpec(keys1.shape, lambda i: (0, 0, 0)),
            pl.BlockSpec(keys2.shape, lambda i: (0, 0, 0)),
        ],
        out_specs=[pl.BlockSpec((tn, N_SEL), lambda i: (i, 0)), pl.BlockSpec((tn, N_SEL), lambda i: (i, 0))],
        out_shape=[jax.ShapeDtypeStruct((n, N_SEL), jnp.int32), jax.ShapeDtypeStruct((n, N_SEL), F32)],
        compiler_params=_params("parallel"),
        name="peer_topk",
    )(q, keys1, keys2)


N_SLOTS = 6
HI16 = -65536
ROW_SUB = 16


def _pack_experts_kernel(u_ref, v_ref, o_ref):
    ub = lax.bitcast_convert_type(u_ref[...].astype(BF16).astype(F32), jnp.int32)
    vb = lax.bitcast_convert_type(v_ref[...].astype(BF16).astype(F32), jnp.int32)
    o_ref[...] = lax.shift_right_logical(ub, 16) | (vb & HI16)


def _pack_experts(exp_u, exp_v, layer, *, tr):
    _, n, d = exp_u.shape
    spec = pl.BlockSpec((None, tr, d), lambda i: (layer, i, 0))
    return pl.pallas_call(
        _pack_experts_kernel,
        grid=(n // tr,),
        in_specs=[spec, spec],
        out_specs=pl.BlockSpec((tr, d), lambda i: (i, 0)),
        out_shape=jax.ShapeDtypeStruct((n, d), jnp.int32),
        compiler_params=_params("parallel"),
        name="pack_experts",
    )(exp_u, exp_v)


def _sublane_sums8(ps):
    sid = lax.broadcasted_iota(jnp.int32, (8, 128), 0)
    m4 = sid < 4
    m2 = (sid & 2) == 0
    m1 = (sid & 1) == 0
    b = [jnp.where(m4, ps[k], ps[k + 4]) + pltpu.roll(jnp.where(m4, ps[k + 4], ps[k]), 4, 0) for k in range(4)]
    c = [jnp.where(m2, b[k], b[k + 2]) + jnp.where(m2, pltpu.roll(b[k], 6, 0), pltpu.roll(b[k + 2], 2, 0))
         for k in range(2)]
    return jnp.where(m1, c[0], c[1]) + jnp.where(m1, pltpu.roll(c[0], 7, 0), pltpu.roll(c[1], 1, 0))


def _peer_experts_kernel(eidx_ref, x_ref, gate_ref, g_ref, gf_ref, uv_hbm, o_ref, buf, sem, ga_ref,
                         *, tb, final_norm):
    d = ROW_SUB * 128
    n_groups = N_SEL // 8

    def issue(t, slot, lo, hi):
        for m in range(lo, hi):
            e = eidx_ref[t, m]
            pltpu.make_async_copy(uv_hbm.at[pl.ds(e, 1)], buf.at[slot, pl.ds(m, 1)], sem.at[slot]).start()

    def wait(slot):
        pltpu.make_async_copy(uv_hbm.at[pl.ds(0, N_SEL)], buf.at[slot], sem.at[slot]).wait()

    eye = (lax.broadcasted_iota(jnp.int32, (N_SEL, N_SEL), 0)
           == lax.broadcasted_iota(jnp.int32, (N_SEL, N_SEL), 1))

    def compute(t, slot, issue_next):
        per = N_SEL // (2 * n_groups)
        x3 = x_ref[t]
        h3 = x3 * lax.rsqrt(jnp.sum(x3 * x3) * (1.0 / d) + EPS) * g_ref[...]
        h0, h1 = h3[0:8], h3[8:16]
        qs = []
        for gi in range(n_groups):
            ps = []
            for j in range(8):
                m = gi * 8 + j
                u0 = lax.bitcast_convert_type(lax.shift_left(buf[slot, m, 0:8, :], 16), F32)
                u1 = lax.bitcast_convert_type(lax.shift_left(buf[slot, m, 8:16, :], 16), F32)
                ps.append(u0 * h0 + u1 * h1)
            qs.append(_sublane_sums8(ps))
            issue_next(gi * per, (gi + 1) * per)
        act = _gelu_tanh(jnp.sum(jnp.concatenate(qs, axis=0), axis=-1, keepdims=True))
        gate_col = jnp.sum(jnp.where(eye, gate_ref[pl.ds(t, 1), :], 0.0), axis=-1, keepdims=True)
        ga_ref[...] = jnp.broadcast_to(gate_col * act, (N_SEL, 128))
        acc0 = jnp.zeros((8, 128), F32)
        acc1 = jnp.zeros((8, 128), F32)
        for m in range(N_SEL):
            gm = jnp.broadcast_to(ga_ref[pl.ds(m, 1), :], (8, 128))
            acc0 = acc0 + gm * lax.bitcast_convert_type(buf[slot, m, 0:8, :] & HI16, F32)
            acc1 = acc1 + gm * lax.bitcast_convert_type(buf[slot, m, 8:16, :] & HI16, F32)
            if m % 8 == 7:
                gi = m // 8
                issue_next(N_SEL // 2 + gi * per, N_SEL // 2 + (gi + 1) * per)
        y3 = x3 + jnp.concatenate([acc0, acc1], axis=0)
        if final_norm:
            y3 = y3 * lax.rsqrt(jnp.sum(y3 * y3) * (1.0 / d) + EPS) * gf_ref[...]
        o_ref[t] = y3

    for t in range(min(N_SLOTS - 1, tb)):
        issue(t, t, 0, N_SEL)

    def body(t, carry):
        slot = lax.rem(t, N_SLOTS)
        nt = t + (N_SLOTS - 1)
        nslot = lax.rem(nt, N_SLOTS)
        wait(slot)

        @pl.when(nt < tb)
        def _():
            compute(t, slot, lambda lo, hi: issue(nt, nslot, lo, hi))

        @pl.when(nt >= tb)
        def _():
            compute(t, slot, lambda lo, hi: None)

        return carry

    lax.fori_loop(0, tb, body, 0)


def _peer_experts(x, eidx, gate, g_ffn, g_final, exp_uv, *, tb, final_norm):
    n, d = x.shape
    slab = (ROW_SUB, 128)
    xspec = pl.BlockSpec((tb,) + slab, lambda i: (i, 0, 0))
    gspec = pl.BlockSpec(slab, lambda i: (0, 0))
    kern = functools.partial(_peer_experts_kernel, tb=tb, final_norm=final_norm)
    out = pl.pallas_call(
        kern,
        grid=(n // tb,),
        in_specs=[
            pl.BlockSpec((tb, N_SEL), lambda i: (i, 0), memory_space=pltpu.SMEM),
            xspec,
            pl.BlockSpec((tb, N_SEL), lambda i: (i, 0)),
            gspec,
            gspec,
            pl.BlockSpec(memory_space=pl.ANY),
        ],
        out_specs=xspec,
        out_shape=jax.ShapeDtypeStruct((n,) + slab, F32),
        scratch_shapes=[
            pltpu.VMEM((N_SLOTS, N_SEL) + slab, jnp.int32),
            pltpu.SemaphoreType.DMA((N_SLOTS,)),
            pltpu.VMEM((N_SEL, 128), F32),
        ],
        compiler_params=_params("arbitrary"),
        name="peer_experts",
    )(eidx, x.reshape((n,) + slab), gate, g_ffn.reshape(slab), g_final.reshape(slab), exp_uv)
    return out.reshape(n, d)


def _layer(x3, k_past, v_past, sc_past, cc_past, w, g_final, final_norm):
    (g_mix, w_in16, w_sconv, w_cconv, b_cconv, g_cnorm, b_cnorm, g_vnorm, b_vnorm,
     w_spatial, b_spatial, w_out, g_ffn, w_query16, keys1, keys2, exp_uv) = w
    b, t, d = x3.shape
    n = b * t
    x = x3.reshape(n, d)
    tm = min(512, n)

    proj = _norm_matmul(x, g_mix, w_in16, tm=tm, tn=1280)
    proj3 = proj.reshape(b, t, proj.shape[1])
    k_new = proj3[:, :, G_WIDTH:2 * G_WIDTH]
    v_new = proj3[:, :, 2 * G_WIDTH:3 * G_WIDTH]
    if k_past is None:
        y_a = _sb_attention(proj3, 0, proj3, 1, proj3, 2, t_q=t, tq=128, tk=128, q_offset=0)
    else:
        past = k_past.shape[1]
        k_all = jnp.concatenate([k_past.reshape(b, past, G_WIDTH), k_new], axis=1)
        v_all = jnp.concatenate([v_past.reshape(b, past, G_WIDTH), v_new], axis=1)
        y_a = _sb_attention(proj3, 0, k_all, 0, v_all, 0, t_q=t, tq=t, tk=t, q_offset=past)
    y_bcd, vn, sc_state, cc_state = _mixers(
        proj3, sc_past, cc_past, w_sconv, w_cconv, b_cconv, g_cnorm, b_cnorm, g_vnorm, b_vnorm,
        w_spatial, b_spatial, tt=min(512, t))
    x1 = _out_proj(x, y_a.reshape(n, G_WIDTH), y_bcd.reshape(n, 3 * G_WIDTH), w_out, tm=min(256, n))
    q = _norm_matmul(x1, g_ffn, w_query16, tm=tm, tn=1024)
    eidx, gate = _peer_topk(q, keys1, keys2, tn=min(256, n))
    x2 = _peer_experts(x1, eidx, gate, g_ffn, g_final, exp_uv, tb=min(64, n), final_norm=final_norm)
    heads = (b, t, N_HEADS, HEAD_DIM)
    return x2.reshape(b, t, d), (k_new.reshape(heads), v_new.reshape(heads), sc_state, cc_state, vn)


def kernel(x_prompt, x_sample, cache_k, cache_v, state_sconv, state_cconv, g_mix, w_in, w_sconv, w_cconv,
           b_cconv, g_cnorm, b_cnorm, g_vnorm, b_vnorm, w_spatial, b_spatial, w_out, g_ffn, w_query,
           sub_keys1, sub_keys2, expert_u, expert_v, g_final):
    depth = w_in.shape[0]
    xp, xs = x_prompt, x_sample
    nb = xp.shape[0]
    outs = [[] for _ in range(9)]
    for l in range(depth):
        w = (g_mix[l], w_in[l].astype(BF16), w_sconv[l], w_cconv[l], b_cconv[l], g_cnorm[l], b_cnorm[l],
             g_vnorm[l], b_vnorm[l], w_spatial[l], b_spatial[l], w_out[l], g_ffn[l],
             w_query[l].astype(BF16), sub_keys1[l], sub_keys2[l],
             _pack_experts(expert_u, expert_v, l, tr=256).reshape(expert_u.shape[1], ROW_SUB, 128))
        last = l == depth - 1
        sc0 = jnp.zeros((nb, SCONV_W - 1, G_WIDTH), F32)
        cc0 = jnp.zeros((nb, CCONV_W - 1, G_WIDTH), F32)
        xp, (k1, v1, sc1, cc1, _) = _layer(xp, None, None, sc0, cc0, w, g_final, last)
        xs, (k2, v2, sc2, cc2, gv2) = _layer(xs, cache_k[l], cache_v[l], state_sconv[l], state_cconv[l],
                                             w, g_final, last)
        for lst, val in zip(outs, (k1, v1, k2, v2, sc1, sc2, cc1, cc2, gv2)):
            lst.append(val)
    kp, vp, ksm, vsm, scp, scs, ccp, ccs, gvs = [jnp.stack(o) for o in outs]
    return (xp, xs, kp, vp, ksm, vsm, scp, scs, ccp, ccs, gvs)
```

```python
import functools

import jax
import jax.numpy as jnp
from jax import lax
from jax.experimental import pallas as pl
from jax.experimental.pallas import tpu as pltpu

F32 = jnp.float32
BF16 = jnp.bfloat16
EPS = 1e-6

G_WIDTH = 512
N_HEADS = 8
HEAD_DIM = 64
SCONV_W = 3
CCONV_W = 31
MLP_CHUNK = 128
PEER_HEADS = 8
PEER_KEYS = 128
PEER_TOPK = 16
PEER_HALF = 128
N_SEL = PEER_HEADS * PEER_TOPK

VMEM_LIMIT_BYTES = 56 * 1024 * 1024


def _params(*sem):
    return pltpu.CompilerParams(dimension_semantics=sem, vmem_limit_bytes=VMEM_LIMIT_BYTES)


def _gelu_tanh(x):
    return 0.5 * x * (1.0 + jnp.tanh(0.7978845608028654 * (x + 0.044715 * (x * x * x))))


def _layernorm(x, g, b):
    xc = x - jnp.mean(x, axis=-1, keepdims=True)
    var = jnp.mean(xc * xc, axis=-1, keepdims=True)
    return xc * lax.rsqrt(var + EPS) * g + b


def _rmsnorm_rows(x, g):
    return x * lax.rsqrt(jnp.mean(x * x, axis=-1, keepdims=True) + EPS) * g


def _norm_matmul_kernel(x_ref, g_ref, w_ref, o_ref, h_ref):
    @pl.when(pl.program_id(1) == 0)
    def _():
        h_ref[...] = _rmsnorm_rows(x_ref[...], g_ref[...]).astype(BF16)

    o_ref[...] = jnp.dot(h_ref[...], w_ref[...], preferred_element_type=F32)


def _norm_matmul(x, g, w, *, tm, tn):
    n, d = x.shape
    nout = w.shape[1]
    return pl.pallas_call(
        _norm_matmul_kernel,
        grid=(n // tm, nout // tn),
        in_specs=[
            pl.BlockSpec((tm, d), lambda i, j: (i, 0)),
            pl.BlockSpec((1, d), lambda i, j: (0, 0)),
            pl.BlockSpec((d, tn), lambda i, j: (0, j)),
        ],
        out_specs=pl.BlockSpec((tm, tn), lambda i, j: (i, j)),
        out_shape=jax.ShapeDtypeStruct((n, nout), F32),
        scratch_shapes=[pltpu.VMEM((tm, d), BF16)],
        compiler_params=_params("parallel", "arbitrary"),
        name="norm_matmul",
    )(x, g.reshape(1, d), w)


RUN_CUTOFF = -104.0


def _sb_kernel(q_ref, k_ref, v_ref, o_ref, run_ref, *, tq, tk, q_offset):
    qi = pl.program_id(1)
    q_pos0 = q_offset + qi * tq
    n_kb = jnp.minimum(pl.cdiv(q_pos0 + tq, tk), k_ref.shape[1] // tk)

    run_ref[...] = jnp.zeros_like(run_ref)
    o_ref[...] = jnp.zeros_like(o_ref)

    row = lax.broadcasted_iota(jnp.int32, (tq, tk), 0)
    col = lax.broadcasted_iota(jnp.int32, (tk, 2 * tk), 1)
    krow = lax.broadcasted_iota(jnp.int32, (tk, 2 * tk), 0)
    suffix_and_total = jnp.where((krow >= col) | (col >= tk), 1.0, 0.0).astype(BF16)
    col_minus_row = lax.broadcasted_iota(jnp.int32, (tq, tk), 1) - row

    def body(carry):
        it, _ = carry
        kb = n_kb - 1 - it
        k0 = pl.multiple_of(kb * tk, tk)
        mask = col_minus_row < (q_pos0 - k0)
        for h in range(N_HEADS):
            hs = slice(h * HEAD_DIM, (h + 1) * HEAD_DIM)
            qh = (q_ref[0, :, hs] * (HEAD_DIM ** -0.5)).astype(BF16)
            kh = k_ref[0, pl.ds(k0, tk), hs].astype(BF16)
            vh = v_ref[0, pl.ds(k0, tk), hs].astype(BF16)
            z = lax.dot_general(qh, kh, (((1,), (1,)), ((), ())), preferred_element_type=F32)
            lf = -(jnp.maximum(z, 0.0) + jnp.log1p(jnp.exp(-jnp.abs(z))))
            lf = jnp.where(mask, lf, 0.0)
            lf_hi = lf.astype(BF16)
            lf_lo = (lf - lf_hi.astype(F32)).astype(BF16)
            sums = (jnp.dot(lf_hi, suffix_and_total, preferred_element_type=F32)
                    + jnp.dot(lf_lo, suffix_and_total, preferred_element_type=F32))
            run = run_ref[h]
            w = jnp.where(mask, jnp.exp(z + sums[:, :tk] + run), 0.0)
            o_ref[0, :, hs] += jnp.dot(w.astype(BF16), vh, preferred_element_type=F32)
            run_ref[h] = run + sums[:, tk:]
        return it + 1, jnp.max(run_ref[...])

    lax.while_loop(lambda c: (c[0] < n_kb) & (c[1] > RUN_CUTOFF), body, (0, 0.0))


def _sb_attention(q_arr, q_col, k_arr, k_col, v_arr, v_col, *, t_q, tq, tk, q_offset):
    b = q_arr.shape[0]
    t_k = k_arr.shape[1]
    kern = functools.partial(_sb_kernel, tq=tq, tk=tk, q_offset=q_offset)
    return pl.pallas_call(
        kern,
        grid=(b, t_q // tq),
        in_specs=[
            pl.BlockSpec((1, tq, G_WIDTH), lambda i, j: (i, j, q_col)),
            pl.BlockSpec((1, t_k, G_WIDTH), lambda i, j: (i, 0, k_col)),
            pl.BlockSpec((1, t_k, G_WIDTH), lambda i, j: (i, 0, v_col)),
        ],
        out_specs=pl.BlockSpec((1, tq, G_WIDTH), lambda i, j: (i, j, 0)),
        out_shape=jax.ShapeDtypeStruct((b, t_q, G_WIDTH), F32),
        scratch_shapes=[pltpu.VMEM((N_HEADS, tq, tk), F32)],
        compiler_params=_params("parallel", "arbitrary"),
        name="sb_attention",
    )(q_arr, k_arr, v_arr)


SC_HIST = 8
CC_HIST = 32


def _mixers_kernel(scb_ref, scc_ref, sch_ref, cfa_ref, cfg_ref, gmu_ref, gmv_ref,
                   sc_past_ref, cc_past_ref, wsc_ref, wcc_ref, bcc_ref, gcn_ref, bcn_ref,
                   gvn_ref, bvn_ref, wsp_ref, bsp_ref,
                   y_ref, vn_ref, sc_state_ref, cc_state_ref, u_buf, g_buf, *, tt, chunk):
    ti = pl.program_id(1)

    @pl.when(ti == 0)
    def _():
        u_buf[0:SC_HIST, :] = jnp.zeros((SC_HIST, G_WIDTH), F32)
        g_buf[0:CC_HIST, :] = jnp.zeros((CC_HIST, G_WIDTH), F32)
        u_buf[SC_HIST - (SCONV_W - 1):SC_HIST, :] = sc_past_ref[0]
        g_buf[CC_HIST - (CCONV_W - 1):CC_HIST, :] = cc_past_ref[0]

    u_buf[SC_HIST:SC_HIST + tt, :] = scc_ref[0] * sch_ref[0]
    conv_b = jnp.zeros((tt, G_WIDTH), F32)
    for k in range(SCONV_W):
        s0 = SC_HIST - (SCONV_W - 1) + k
        conv_b = conv_b + wsc_ref[k:k + 1, :] * u_buf[s0:s0 + tt, :]
    y_ref[0, :, 0:G_WIDTH] = scb_ref[0] * conv_b

    g_buf[CC_HIST:CC_HIST + tt, :] = cfa_ref[0] * jax.nn.sigmoid(cfg_ref[0])
    conv_c = jnp.zeros((tt, G_WIDTH), F32)
    for k in range(CCONV_W):
        s0 = CC_HIST - (CCONV_W - 1) + k
        conv_c = conv_c + wcc_ref[k:k + 1, :] * g_buf[s0:s0 + tt, :]
    ln = _layernorm(conv_c + bcc_ref[...], gcn_ref[...], bcn_ref[...])
    y_ref[0, :, G_WIDTH:2 * G_WIDTH] = ln * jax.nn.sigmoid(ln)

    vn = _layernorm(_gelu_tanh(gmv_ref[0]), gvn_ref[...], bvn_ref[...])
    vn_ref[0] = vn
    vn16 = vn.astype(BF16)
    for c in range(tt // chunk):
        rows = slice(c * chunk, (c + 1) * chunk)
        parts = []
        for h in range(N_HEADS):
            hs = slice(h * HEAD_DIM, (h + 1) * HEAD_DIM)
            parts.append(jnp.dot(wsp_ref[h], vn16[rows, hs], preferred_element_type=F32))
        s = jnp.concatenate(parts, axis=-1) + bsp_ref[...]
        y_ref[0, rows, 2 * G_WIDTH:3 * G_WIDTH] = _gelu_tanh(gmu_ref[0, rows, :]) * s

    sc_state_ref[0] = u_buf[SC_HIST + tt - (SCONV_W - 1):SC_HIST + tt, :]
    cc_state_ref[0] = g_buf[CC_HIST + tt - (CCONV_W - 1):CC_HIST + tt, :]
    u_buf[0:SC_HIST, :] = u_buf[tt:tt + SC_HIST, :]
    g_buf[0:CC_HIST, :] = g_buf[tt:tt + CC_HIST, :]


def _mixers(proj3, sc_past, cc_past, w_sconv, w_cconv, b_cconv, g_cnorm, b_cnorm, g_vnorm, b_vnorm,
            w_spatial, b_spatial, *, tt):
    b, t, _ = proj3.shape
    chunk = min(t, MLP_CHUNK)
    wsp = jnp.tril(w_spatial[:, :chunk, :chunk]).astype(BF16)
    bsp = jnp.repeat(b_spatial[:, :chunk].T, HEAD_DIM, axis=1)

    def col(c):
        return pl.BlockSpec((1, tt, G_WIDTH), lambda i, j: (i, j, c))

    def whole(a):
        return pl.BlockSpec(a.shape, lambda i, j: (0,) * a.ndim)

    def per_batch(a):
        return pl.BlockSpec((1,) + a.shape[1:], lambda i, j: (i,) + (0,) * (a.ndim - 1))

    row = lambda a: a.reshape(1, G_WIDTH)
    small = [w_sconv, w_cconv, row(b_cconv), row(g_cnorm), row(b_cnorm), row(g_vnorm), row(b_vnorm), wsp, bsp]
    kern = functools.partial(_mixers_kernel, tt=tt, chunk=chunk)
    return pl.pallas_call(
        kern,
        grid=(b, t // tt),
        in_specs=[col(c) for c in range(3, 10)] + [per_batch(sc_past), per_batch(cc_past)]
                 + [whole(a) for a in small],
        out_specs=[
            pl.BlockSpec((1, tt, 3 * G_WIDTH), lambda i, j: (i, j, 0)),
            pl.BlockSpec((1, tt, G_WIDTH), lambda i, j: (i, j, 0)),
            pl.BlockSpec((1, SCONV_W - 1, G_WIDTH), lambda i, j: (i, 0, 0)),
            pl.BlockSpec((1, CCONV_W - 1, G_WIDTH), lambda i, j: (i, 0, 0)),
        ],
        out_shape=[
            jax.ShapeDtypeStruct((b, t, 3 * G_WIDTH), F32),
            jax.ShapeDtypeStruct((b, t, G_WIDTH), F32),
            jax.ShapeDtypeStruct((b, SCONV_W - 1, G_WIDTH), F32),
            jax.ShapeDtypeStruct((b, CCONV_W - 1, G_WIDTH), F32),
        ],
        scratch_shapes=[pltpu.VMEM((SC_HIST + tt, G_WIDTH), F32), pltpu.VMEM((CC_HIST + tt, G_WIDTH), F32)],
        compiler_params=_params("parallel", "arbitrary"),
        name="mixers",
    )(*([proj3] * 7), sc_past, cc_past, *small)


def _out_proj_kernel(x_ref, ya_ref, yb_ref, wa_ref, wb_ref, o_ref):
    o_ref[...] = (x_ref[...]
                  + jnp.dot(ya_ref[...].astype(BF16), wa_ref[...], preferred_element_type=F32)
                  + jnp.dot(yb_ref[...].astype(BF16), wb_ref[...], preferred_element_type=F32))


def _out_proj(x, ya, yb, w_out, *, tm):
    n, d = x.shape
    wa = w_out[:G_WIDTH].astype(BF16)
    wb = w_out[G_WIDTH:].astype(BF16)
    return pl.pallas_call(
        _out_proj_kernel,
        grid=(n // tm,),
        in_specs=[
            pl.BlockSpec((tm, d), lambda i: (i, 0)),
            pl.BlockSpec((tm, G_WIDTH), lambda i: (i, 0)),
            pl.BlockSpec((tm, 3 * G_WIDTH), lambda i: (i, 0)),
            pl.BlockSpec(wa.shape, lambda i: (0, 0)),
            pl.BlockSpec(wb.shape, lambda i: (0, 0)),
        ],
        out_specs=pl.BlockSpec((tm, d), lambda i: (i, 0)),
        out_shape=jax.ShapeDtypeStruct((n, d), F32),
        compiler_params=_params("parallel"),
        name="out_proj",
    )(x, ya, yb, wa, wb)


def _top16_rows(s, rid, rid_end):
    vals, idxs = [], []
    for _ in range(PEER_TOPK):
        m = jnp.max(s, axis=0, keepdims=True)
        i = jnp.min(jnp.where(s == m, rid, rid_end), axis=0, keepdims=True)
        s = jnp.where(rid == i, -jnp.inf, s)
        vals.append(m)
        idxs.append(i)
    return jnp.concatenate(vals, axis=0), jnp.concatenate(idxs, axis=0)


CAND_B = 8
CAND_ROWS = PEER_TOPK + (CAND_B - 1) * CAND_B + (PEER_TOPK - CAND_B)


def _pair_rows(x1, x2, combine):
    rows = [combine(x1[0:1], x2)]
    rows += [combine(x1[a:a + 1], x2[0:CAND_B]) for a in range(1, CAND_B)]
    rows.append(combine(x1[CAND_B:], x2[0:1]))
    return jnp.concatenate(rows, axis=0)


def _peer_topk_kernel(q_ref, k1_ref, k2_ref, eidx_ref, gate_ref):
    nt = (((1,), (1,)), ((), ()))
    tn = q_ref.shape[0]
    key_id = lax.broadcasted_iota(jnp.int32, (PEER_KEYS, tn), 0).astype(F32)
    r = lax.broadcasted_iota(jnp.int32, (CAND_ROWS, tn), 0)
    mid = r - PEER_TOPK
    last = r - (CAND_ROWS - (PEER_TOPK - CAND_B))
    flat_id = jnp.where(r < PEER_TOPK, r,
                        jnp.where(last < 0, ((mid >> 3) + 1) * PEER_TOPK + (mid & (CAND_B - 1)),
                                  (last + CAND_B) * PEER_TOPK)).astype(F32)
    assert CAND_B == 8
    e_rows, g_rows = [], []
    for h in range(PEER_HEADS):
        qa = q_ref[:, (2 * h) * PEER_HALF:(2 * h + 1) * PEER_HALF].astype(BF16)
        qb = q_ref[:, (2 * h + 1) * PEER_HALF:(2 * h + 2) * PEER_HALF].astype(BF16)
        s1 = lax.dot_general(k1_ref[h].astype(BF16), qa, nt, preferred_element_type=F32)
        s2 = lax.dot_general(k2_ref[h].astype(BF16), qb, nt, preferred_element_type=F32)
        t1, i1 = _top16_rows(s1, key_id, float(PEER_KEYS))
        t2, i2 = _top16_rows(s2, key_id, float(PEER_KEYS))
        cand = _pair_rows(t1, t2, lambda a, b: a + b)
        cidx = _pair_rows(i1, i2, lambda a, b: a * float(PEER_KEYS) + b)
        top, sel = _top16_rows(cand, flat_id, float(PEER_TOPK * PEER_TOPK))
        eidx = [jnp.sum(jnp.where(flat_id == sel[r:r + 1], cidx, 0.0), axis=0, keepdims=True)
                for r in range(PEER_TOPK)]
        p = jnp.exp(top - top[0:1])
        g_rows.append(p / jnp.sum(p, axis=0, keepdims=True))
        e_rows.append(jnp.concatenate(eidx, axis=0))
    eidx_ref[...] = jnp.concatenate(e_rows, axis=0).T.astype(jnp.int32)
    gate_ref[...] = jnp.concatenate(g_rows, axis=0).T


def _peer_topk(q, keys1, keys2, *, tn):
    n, d = q.shape
    return pl.pallas_call(
        _peer_topk_kernel,
        grid=(n // tn,),
        in_specs=[
            pl.BlockSpec((tn, d), lambda i: (i, 0)),
            pl.BlockSpec(keys1.shape, lambda i: (0, 0, 0)),
            pl.BlockSpec(keys2.shape, lambda i: (0, 0, 0)),
        ],
        out_specs=[pl.BlockSpec((tn, N_SEL), lambda i: (i, 0)), pl.BlockSpec((tn, N_SEL), lambda i: (i, 0))],
        out_shape=[jax.ShapeDtypeStruct((n, N_SEL), jnp.int32), jax.ShapeDtypeStruct((n, N_SEL), F32)],
        compiler_params=_params("parallel"),
        name="peer_topk",
    )(q, keys1, keys2)


N_SLOTS = 6
HI16 = -65536
ROW_SUB = 16


def _pack_experts_kernel(u_ref, v_ref, o_ref):
    ub = lax.bitcast_convert_type(u_ref[...].astype(BF16).astype(F32), jnp.int32)
    vb = lax.bitcast_convert_type(v_ref[...].astype(BF16).astype(F32), jnp.int32)
    o_ref[...] = lax.shift_right_logical(ub, 16) | (vb & HI16)


def _pack_experts(exp_u, exp_v, layer, *, tr):
    _, n, d = exp_u.shape
    spec = pl.BlockSpec((None, tr, d), lambda i: (layer, i, 0))
    return pl.pallas_call(
        _pack_experts_kernel,
        grid=(n // tr,),
        in_specs=[spec, spec],
        out_specs=pl.BlockSpec((tr, d), lambda i: (i, 0)),
        out_shape=jax.ShapeDtypeStruct((n, d), jnp.int32),
        compiler_params=_params("parallel"),
        name="pack_experts",
    )(exp_u, exp_v)


def _sublane_sums8(ps):
    sid = lax.broadcasted_iota(jnp.int32, (8, 128), 0)
    m4 = sid < 4
    m2 = (sid & 2) == 0
    m1 = (sid & 1) == 0
    b = [jnp.where(m4, ps[k], ps[k + 4]) + pltpu.roll(jnp.where(m4, ps[k + 4], ps[k]), 4, 0) for k in range(4)]
    c = [jnp.where(m2, b[k], b[k + 2]) + jnp.where(m2, pltpu.roll(b[k], 6, 0), pltpu.roll(b[k + 2], 2, 0))
         for k in range(2)]
    return jnp.where(m1, c[0], c[1]) + jnp.where(m1, pltpu.roll(c[0], 7, 0), pltpu.roll(c[1], 1, 0))


def _block_rms_scale(x):
    ss = jnp.sum(jnp.sum(x * x, axis=2, keepdims=True), axis=1, keepdims=True)
    return lax.rsqrt(ss * (1.0 / (ROW_SUB * 128)) + EPS)


def _peer_experts_kernel(eidx_ref, x_ref, gate_ref, g_ref, gf_ref, uv_hbm, o_ref, buf, sem, ga_ref, h_ref,
                         *, tb, final_norm):
    n_groups = N_SEL // 8
    per = N_SEL // (2 * n_groups)
    lookahead = N_SLOTS - 1

    def issue(t, lo, hi):
        slot = lax.rem(t, N_SLOTS)
        for m in range(lo, hi):
            e = eidx_ref[t, m]
            pltpu.make_async_copy(uv_hbm.at[pl.ds(e, 1)], buf.at[slot, pl.ds(m, 1)], sem.at[slot]).start()

    def wait(t):
        slot = lax.rem(t, N_SLOTS)
        pltpu.make_async_copy(uv_hbm.at[pl.ds(0, N_SEL)], buf.at[slot], sem.at[slot]).wait()

    eye = (lax.broadcasted_iota(jnp.int32, (N_SEL, N_SEL), 0)
           == lax.broadcasted_iota(jnp.int32, (N_SEL, N_SEL), 1))

    def stage_a_groups(t, issue_next):
        slot = lax.rem(t, N_SLOTS)
        h0, h1 = h_ref[t, 0:8, :], h_ref[t, 8:16, :]
        qs = []
        for gi in range(n_groups):
            issue_next(gi * per, (gi + 1) * per)
            ps = []
            for j in range(8):
                m = gi * 8 + j
                u0 = lax.bitcast_convert_type(lax.shift_left(buf[slot, m, 0:8, :], 16), F32)
                u1 = lax.bitcast_convert_type(lax.shift_left(buf[slot, m, 8:16, :], 16), F32)
                ps.append(u0 * h0 + u1 * h1)
            qs.append(_sublane_sums8(ps))
        return qs

    def stage_a_tail(t, qs):
        act = _gelu_tanh(jnp.sum(jnp.concatenate(qs, axis=0), axis=-1, keepdims=True))
        gate_col = jnp.sum(jnp.where(eye, gate_ref[pl.ds(t, 1), :], 0.0), axis=-1, keepdims=True)
        ga_ref[lax.rem(t, 2)] = jnp.broadcast_to(gate_col * act, (N_SEL, 128))

    def stage_b(t, issue_next):
        slot = lax.rem(t, N_SLOTS)
        par = lax.rem(t, 2)
        acc0 = jnp.zeros((8, 128), F32)
        acc1 = jnp.zeros((8, 128), F32)
        for m in range(N_SEL):
            if m % 8 == 0:
                gi = m // 8
                issue_next(N_SEL // 2 + gi * per, N_SEL // 2 + (gi + 1) * per)
            gm = jnp.broadcast_to(ga_ref[par, pl.ds(m, 1), :], (8, 128))
            acc0 = acc0 + gm * lax.bitcast_convert_type(buf[slot, m, 0:8, :] & HI16, F32)
            acc1 = acc1 + gm * lax.bitcast_convert_type(buf[slot, m, 8:16, :] & HI16, F32)
        o_ref[t] = x_ref[t] + jnp.concatenate([acc0, acc1], axis=0)

    def no_issue(lo, hi):
        del lo, hi

    for t in range(lookahead):
        issue(t, 0, N_SEL)
    xb = x_ref[...]
    h_ref[...] = xb * _block_rms_scale(xb) * g_ref[...]
    wait(0)
    stage_a_tail(0, stage_a_groups(0, no_issue))

    def steady(t, carry):
        wait(t + 1)
        issue_next = lambda lo, hi: issue(t + lookahead, lo, hi)
        qs = stage_a_groups(t + 1, issue_next)
        stage_b(t, issue_next)
        stage_a_tail(t + 1, qs)
        return carry

    def drain(t, carry):
        wait(t + 1)
        qs = stage_a_groups(t + 1, no_issue)
        stage_b(t, no_issue)
        stage_a_tail(t + 1, qs)
        return carry

    lax.fori_loop(0, tb - lookahead, steady, 0)
    lax.fori_loop(tb - lookahead, tb - 1, drain, 0)
    stage_b(tb - 1, no_issue)
    if final_norm:
        yb = o_ref[...]
        o_ref[...] = yb * _block_rms_scale(yb) * gf_ref[...]


def _peer_experts(x, eidx, gate, g_ffn, g_final, exp_uv, *, tb, final_norm):
    n, d = x.shape
    slab = (ROW_SUB, 128)
    xspec = pl.BlockSpec((tb,) + slab, lambda i: (i, 0, 0))
    gspec = pl.BlockSpec(slab, lambda i: (0, 0))
    kern = functools.partial(_peer_experts_kernel, tb=tb, final_norm=final_norm)
    out = pl.pallas_call(
        kern,
        grid=(n // tb,),
        in_specs=[
            pl.BlockSpec((tb, N_SEL), lambda i: (i, 0), memory_space=pltpu.SMEM),
            xspec,
            pl.BlockSpec((tb, N_SEL), lambda i: (i, 0)),
            gspec,
            gspec,
            pl.BlockSpec(memory_space=pl.ANY),
        ],
        out_specs=xspec,
        out_shape=jax.ShapeDtypeStruct((n,) + slab, F32),
        scratch_shapes=[
            pltpu.VMEM((N_SLOTS, N_SEL) + slab, jnp.int32),
            pltpu.SemaphoreType.DMA((N_SLOTS,)),
            pltpu.VMEM((2, N_SEL, 128), F32),
            pltpu.VMEM((tb,) + slab, F32),
        ],
        compiler_params=_params("arbitrary"),
        name="peer_experts",
    )(eidx, x.reshape((n,) + slab), gate, g_ffn.reshape(slab), g_final.reshape(slab), exp_uv)
    return out.reshape(n, d)


def _layer(x3, k_past, v_past, sc_past, cc_past, w, g_final, final_norm):
    (g_mix, w_in16, w_sconv, w_cconv, b_cconv, g_cnorm, b_cnorm, g_vnorm, b_vnorm,
     w_spatial, b_spatial, w_out, g_ffn, w_query16, keys1, keys2, exp_uv) = w
    b, t, d = x3.shape
    n = b * t
    x = x3.reshape(n, d)
    tm = min(512, n)

    proj = _norm_matmul(x, g_mix, w_in16, tm=tm, tn=1280)
    proj3 = proj.reshape(b, t, proj.shape[1])
    k_new = proj3[:, :, G_WIDTH:2 * G_WIDTH]
    v_new = proj3[:, :, 2 * G_WIDTH:3 * G_WIDTH]
    if k_past is None:
        y_a = _sb_attention(proj3, 0, proj3, 1, proj3, 2, t_q=t, tq=128, tk=128, q_offset=0)
    else:
        past = k_past.shape[1]
        k_all = jnp.concatenate([k_past.reshape(b, past, G_WIDTH), k_new], axis=1)
        v_all = jnp.concatenate([v_past.reshape(b, past, G_WIDTH), v_new], axis=1)
        y_a = _sb_attention(proj3, 0, k_all, 0, v_all, 0, t_q=t, tq=t, tk=t, q_offset=past)
    y_bcd, vn, sc_state, cc_state = _mixers(
        proj3, sc_past, cc_past, w_sconv, w_cconv, b_cconv, g_cnorm, b_cnorm, g_vnorm, b_vnorm,
        w_spatial, b_spatial, tt=min(512, t))
    x1 = _out_proj(x, y_a.reshape(n, G_WIDTH), y_bcd.reshape(n, 3 * G_WIDTH), w_out, tm=min(256, n))
    q = _norm_matmul(x1, g_ffn, w_query16, tm=tm, tn=1024)
    eidx, gate = _peer_topk(q, keys1, keys2, tn=min(512, n))
    x2 = _peer_experts(x1, eidx, gate, g_ffn, g_final, exp_uv, tb=min(128, n), final_norm=final_norm)
    heads = (b, t, N_HEADS, HEAD_DIM)
    return x2.reshape(b, t, d), (k_new.reshape(heads), v_new.reshape(heads), sc_state, cc_state, vn)


def kernel(x_prompt, x_sample, cache_k, cache_v, state_sconv, state_cconv, g_mix, w_in, w_sconv, w_cconv,
           b_cconv, g_cnorm, b_cnorm, g_vnorm, b_vnorm, w_spatial, b_spatial, w_out, g_ffn, w_query,
           sub_keys1, sub_keys2, expert_u, expert_v, g_final):
    depth = w_in.shape[0]
    xp, xs = x_prompt, x_sample
    nb = xp.shape[0]
    outs = [[] for _ in range(9)]
    for l in range(depth):
        w = (g_mix[l], w_in[l].astype(BF16), w_sconv[l], w_cconv[l], b_cconv[l], g_cnorm[l], b_cnorm[l],
             g_vnorm[l], b_vnorm[l], w_spatial[l], b_spatial[l], w_out[l], g_ffn[l],
             w_query[l].astype(BF16), sub_keys1[l], sub_keys2[l],
             _pack_experts(expert_u, expert_v, l, tr=256).reshape(expert_u.shape[1], ROW_SUB, 128))
        last = l == depth - 1
        sc0 = jnp.zeros((nb, SCONV_W - 1, G_WIDTH), F32)
        cc0 = jnp.zeros((nb, CCONV_W - 1, G_WIDTH), F32)
        xp, (k1, v1, sc1, cc1, _) = _layer(xp, None, None, sc0, cc0, w, g_final, last)
        xs, (k2, v2, sc2, cc2, gv2) = _layer(xs, cache_k[l], cache_v[l], state_sconv[l], state_cconv[l],
                                             w, g_final, last)
        for lst, val in zip(outs, (k1, v1, k2, v2, sc1, sc2, cc1, cc2, gv2)):
            lst.append(val)
    kp, vp, ksm, vsm, scp, scs, ccp, ccs, gvs = [jnp.stack(o) for o in outs]
    return (xp, xs, kp, vp, ksm, vsm, scp, scs, ccp, ccs, gvs)
```

```python
import functools

import jax
import jax.numpy as jnp
from jax import lax
from jax.experimental import pallas as pl
from jax.experimental.pallas import tpu as pltpu

F32 = jnp.float32
BF16 = jnp.bfloat16
EPS = 1e-6

G_WIDTH = 512
N_HEADS = 8
HEAD_DIM = 64
SCONV_W = 3
CCONV_W = 31
MLP_CHUNK = 128
PEER_HEADS = 8
PEER_KEYS = 128
PEER_TOPK = 16
PEER_HALF = 128
N_SEL = PEER_HEADS * PEER_TOPK

VMEM_LIMIT_BYTES = 56 * 1024 * 1024


def _params(*sem):
    return pltpu.CompilerParams(dimension_semantics=sem, vmem_limit_bytes=VMEM_LIMIT_BYTES)


def _gelu_tanh(x):
    return 0.5 * x * (1.0 + jnp.tanh(0.7978845608028654 * (x + 0.044715 * (x * x * x))))


def _layernorm(x, g, b):
    xc = x - jnp.mean(x, axis=-1, keepdims=True)
    var = jnp.mean(xc * xc, axis=-1, keepdims=True)
    return xc * lax.rsqrt(var + EPS) * g + b


def _rmsnorm_rows(x, g):
    return x * lax.rsqrt(jnp.mean(x * x, axis=-1, keepdims=True) + EPS) * g


def _norm_matmul_kernel(x_ref, g_ref, w_ref, o_ref, h_ref):
    @pl.when(pl.program_id(1) == 0)
    def _():
        h_ref[...] = _rmsnorm_rows(x_ref[...], g_ref[...]).astype(BF16)

    o_ref[...] = jnp.dot(h_ref[...], w_ref[...], preferred_element_type=F32)


def _norm_matmul(x, g, w, *, tm, tn):
    n, d = x.shape
    nout = w.shape[1]
    return pl.pallas_call(
        _norm_matmul_kernel,
        grid=(n // tm, nout // tn),
        in_specs=[
            pl.BlockSpec((tm, d), lambda i, j: (i, 0)),
            pl.BlockSpec((1, d), lambda i, j: (0, 0)),
            pl.BlockSpec((d, tn), lambda i, j: (0, j)),
        ],
        out_specs=pl.BlockSpec((tm, tn), lambda i, j: (i, j)),
        out_shape=jax.ShapeDtypeStruct((n, nout), F32),
        scratch_shapes=[pltpu.VMEM((tm, d), BF16)],
        compiler_params=_params("parallel", "arbitrary"),
        name="norm_matmul",
    )(x, g.reshape(1, d), w)


RUN_CUTOFF = -104.0


def _sb_kernel(q_ref, k_ref, v_ref, o_ref, run_ref, acc_ref, q16_ref, *, tq, tk, q_offset):
    qi = pl.program_id(1)
    q_pos0 = q_offset + qi * tq
    n_kb = jnp.minimum(pl.cdiv(q_pos0 + tq, tk), k_ref.shape[1] // tk)

    run_ref[...] = jnp.zeros_like(run_ref)
    acc_ref[...] = jnp.zeros_like(acc_ref)
    for h in range(N_HEADS):
        q16_ref[h] = (q_ref[0, :, h * HEAD_DIM:(h + 1) * HEAD_DIM] * (HEAD_DIM ** -0.5)).astype(BF16)

    rows = N_HEADS * tq
    row = lax.broadcasted_iota(jnp.int32, (rows, tk), 0) & (tq - 1)
    col_minus_row = lax.broadcasted_iota(jnp.int32, (rows, tk), 1) - row
    col = lax.broadcasted_iota(jnp.int32, (2 * tk, 2 * tk), 1)
    krow = lax.broadcasted_iota(jnp.int32, (2 * tk, 2 * tk), 0) & (tk - 1)
    suffix_and_total = jnp.where((krow >= col) | (col >= tk), 1.0, 0.0).astype(BF16)

    def body(carry):
        it, _ = carry
        kb = n_kb - 1 - it
        k0 = pl.multiple_of(kb * tk, tk)
        mask = col_minus_row < (q_pos0 - k0)
        zs = []
        for h in range(N_HEADS):
            kh = k_ref[0, pl.ds(k0, tk), h * HEAD_DIM:(h + 1) * HEAD_DIM].astype(BF16)
            zs.append(lax.dot_general(q16_ref[h], kh, (((1,), (1,)), ((), ())), preferred_element_type=F32))
        z = jnp.concatenate(zs, axis=0)
        lf = -(jnp.maximum(z, 0.0) + jnp.log1p(jnp.exp(-jnp.abs(z))))
        lf = jnp.where(mask, lf, 0.0)
        lf_hi = lf.astype(BF16)
        lf_lo = (lf - lf_hi.astype(F32)).astype(BF16)
        sums = jnp.dot(jnp.concatenate([lf_hi, lf_lo], axis=1), suffix_and_total, preferred_element_type=F32)
        run = run_ref[...]
        w = jnp.where(mask, jnp.exp(z + sums[:, :tk] + run), 0.0).astype(BF16)
        for h in range(N_HEADS):
            vh = v_ref[0, pl.ds(k0, tk), h * HEAD_DIM:(h + 1) * HEAD_DIM].astype(BF16)
            acc_ref[h] += jnp.dot(w[h * tq:(h + 1) * tq], vh, preferred_element_type=F32)
        run = run + sums[:, tk:]
        run_ref[...] = run
        return it + 1, jnp.max(run)

    lax.while_loop(lambda c: (c[0] < n_kb) & (c[1] > RUN_CUTOFF), body, (0, 0.0))
    o_ref[0] = jnp.concatenate([acc_ref[h] for h in range(N_HEADS)], axis=-1)


def _sb_attention(q_arr, q_col, k_arr, k_col, v_arr, v_col, *, t_q, tq, tk, q_offset):
    b = q_arr.shape[0]
    t_k = k_arr.shape[1]
    kern = functools.partial(_sb_kernel, tq=tq, tk=tk, q_offset=q_offset)
    return pl.pallas_call(
        kern,
        grid=(b, t_q // tq),
        in_specs=[
            pl.BlockSpec((1, tq, G_WIDTH), lambda i, j: (i, j, q_col)),
            pl.BlockSpec((1, t_k, G_WIDTH), lambda i, j: (i, 0, k_col)),
            pl.BlockSpec((1, t_k, G_WIDTH), lambda i, j: (i, 0, v_col)),
        ],
        out_specs=pl.BlockSpec((1, tq, G_WIDTH), lambda i, j: (i, j, 0)),
        out_shape=jax.ShapeDtypeStruct((b, t_q, G_WIDTH), F32),
        scratch_shapes=[
            pltpu.VMEM((N_HEADS * tq, tk), F32),
            pltpu.VMEM((N_HEADS, tq, HEAD_DIM), F32),
            pltpu.VMEM((N_HEADS, tq, HEAD_DIM), BF16),
        ],
        compiler_params=_params("parallel", "arbitrary"),
        name="sb_attention",
    )(q_arr, k_arr, v_arr)


SC_HIST = 8
CC_HIST = 32


def _mixers_kernel(scb_ref, scc_ref, sch_ref, cfa_ref, cfg_ref, gmu_ref, gmv_ref,
                   sc_past_ref, cc_past_ref, wsc_ref, wcc_ref, bcc_ref, gcn_ref, bcn_ref,
                   gvn_ref, bvn_ref, wsp_ref, bsp_ref,
                   y_ref, vn_ref, sc_state_ref, cc_state_ref, u_buf, g_buf, *, tt, chunk):
    ti = pl.program_id(1)

    @pl.when(ti == 0)
    def _():
        u_buf[0:SC_HIST, :] = jnp.zeros((SC_HIST, G_WIDTH), F32)
        g_buf[0:CC_HIST, :] = jnp.zeros((CC_HIST, G_WIDTH), F32)
        u_buf[SC_HIST - (SCONV_W - 1):SC_HIST, :] = sc_past_ref[0]
        g_buf[CC_HIST - (CCONV_W - 1):CC_HIST, :] = cc_past_ref[0]

    u_buf[SC_HIST:SC_HIST + tt, :] = scc_ref[0] * sch_ref[0]
    conv_b = jnp.zeros((tt, G_WIDTH), F32)
    for k in range(SCONV_W):
        s0 = SC_HIST - (SCONV_W - 1) + k
        conv_b = conv_b + wsc_ref[k:k + 1, :] * u_buf[s0:s0 + tt, :]
    y_ref[0, :, 0:G_WIDTH] = scb_ref[0] * conv_b

    g_buf[CC_HIST:CC_HIST + tt, :] = cfa_ref[0] * jax.nn.sigmoid(cfg_ref[0])
    conv_c = jnp.zeros((tt, G_WIDTH), F32)
    for k in range(CCONV_W):
        s0 = CC_HIST - (CCONV_W - 1) + k
        conv_c = conv_c + wcc_ref[k:k + 1, :] * g_buf[s0:s0 + tt, :]
    ln = _layernorm(conv_c + bcc_ref[...], gcn_ref[...], bcn_ref[...])
    y_ref[0, :, G_WIDTH:2 * G_WIDTH] = ln * jax.nn.sigmoid(ln)

    vn = _layernorm(_gelu_tanh(gmv_ref[0]), gvn_ref[...], bvn_ref[...])
    vn_ref[0] = vn
    vn16 = vn.astype(BF16)
    for c in range(tt // chunk):
        rows = slice(c * chunk, (c + 1) * chunk)
        parts = []
        for h in range(N_HEADS):
            hs = slice(h * HEAD_DIM, (h + 1) * HEAD_DIM)
            parts.append(jnp.dot(wsp_ref[h], vn16[rows, hs], preferred_element_type=F32))
        s = jnp.concatenate(parts, axis=-1) + bsp_ref[...]
        y_ref[0, rows, 2 * G_WIDTH:3 * G_WIDTH] = _gelu_tanh(gmu_ref[0, rows, :]) * s

    sc_state_ref[0] = u_buf[SC_HIST + tt - (SCONV_W - 1):SC_HIST + tt, :]
    cc_state_ref[0] = g_buf[CC_HIST + tt - (CCONV_W - 1):CC_HIST + tt, :]
    u_buf[0:SC_HIST, :] = u_buf[tt:tt + SC_HIST, :]
    g_buf[0:CC_HIST, :] = g_buf[tt:tt + CC_HIST, :]


def _mixers(proj3, sc_past, cc_past, w_sconv, w_cconv, b_cconv, g_cnorm, b_cnorm, g_vnorm, b_vnorm,
            w_spatial, b_spatial, *, tt):
    b, t, _ = proj3.shape
    chunk = min(t, MLP_CHUNK)
    wsp = jnp.tril(w_spatial[:, :chunk, :chunk]).astype(BF16)
    bsp = jnp.repeat(b_spatial[:, :chunk].T, HEAD_DIM, axis=1)

    def col(c):
        return pl.BlockSpec((1, tt, G_WIDTH), lambda i, j: (i, j, c))

    def whole(a):
        return pl.BlockSpec(a.shape, lambda i, j: (0,) * a.ndim)

    def per_batch(a):
        return pl.BlockSpec((1,) + a.shape[1:], lambda i, j: (i,) + (0,) * (a.ndim - 1))

    row = lambda a: a.reshape(1, G_WIDTH)
    small = [w_sconv, w_cconv, row(b_cconv), row(g_cnorm), row(b_cnorm), row(g_vnorm), row(b_vnorm), wsp, bsp]
    kern = functools.partial(_mixers_kernel, tt=tt, chunk=chunk)
    return pl.pallas_call(
        kern,
        grid=(b, t // tt),
        in_specs=[col(c) for c in range(3, 10)] + [per_batch(sc_past), per_batch(cc_past)]
                 + [whole(a) for a in small],
        out_specs=[
            pl.BlockSpec((1, tt, 3 * G_WIDTH), lambda i, j: (i, j, 0)),
            pl.BlockSpec((1, tt, G_WIDTH), lambda i, j: (i, j, 0)),
            pl.BlockSpec((1, SCONV_W - 1, G_WIDTH), lambda i, j: (i, 0, 0)),
            pl.BlockSpec((1, CCONV_W - 1, G_WIDTH), lambda i, j: (i, 0, 0)),
        ],
        out_shape=[
            jax.ShapeDtypeStruct((b, t, 3 * G_WIDTH), F32),
            jax.ShapeDtypeStruct((b, t, G_WIDTH), F32),
            jax.ShapeDtypeStruct((b, SCONV_W - 1, G_WIDTH), F32),
            jax.ShapeDtypeStruct((b, CCONV_W - 1, G_WIDTH), F32),
        ],
        scratch_shapes=[pltpu.VMEM((SC_HIST + tt, G_WIDTH), F32), pltpu.VMEM((CC_HIST + tt, G_WIDTH), F32)],
        compiler_params=_params("parallel", "arbitrary"),
        name="mixers",
    )(*([proj3] * 7), sc_past, cc_past, *small)


def _out_proj_kernel(x_ref, ya_ref, yb_ref, wa_ref, wb_ref, o_ref):
    o_ref[...] = (x_ref[...]
                  + jnp.dot(ya_ref[...].astype(BF16), wa_ref[...], preferred_element_type=F32)
                  + jnp.dot(yb_ref[...].astype(BF16), wb_ref[...], preferred_element_type=F32))


def _out_proj(x, ya, yb, w_out, *, tm):
    n, d = x.shape
    wa = w_out[:G_WIDTH].astype(BF16)
    wb = w_out[G_WIDTH:].astype(BF16)
    return pl.pallas_call(
        _out_proj_kernel,
        grid=(n // tm,),
        in_specs=[
            pl.BlockSpec((tm, d), lambda i: (i, 0)),
            pl.BlockSpec((tm, G_WIDTH), lambda i: (i, 0)),
            pl.BlockSpec((tm, 3 * G_WIDTH), lambda i: (i, 0)),
            pl.BlockSpec(wa.shape, lambda i: (0, 0)),
            pl.BlockSpec(wb.shape, lambda i: (0, 0)),
        ],
        out_specs=pl.BlockSpec((tm, d), lambda i: (i, 0)),
        out_shape=jax.ShapeDtypeStruct((n, d), F32),
        compiler_params=_params("parallel"),
        name="out_proj",
    )(x, ya, yb, wa, wb)


def _top16_rows(s, rid, rid_end):
    vals, idxs = [], []
    for _ in range(PEER_TOPK):
        m = jnp.max(s, axis=0, keepdims=True)
        i = jnp.min(jnp.where(s == m, rid, rid_end), axis=0, keepdims=True)
        s = jnp.where(rid == i, -jnp.inf, s)
        vals.append(m)
        idxs.append(i)
    return jnp.concatenate(vals, axis=0), jnp.concatenate(idxs, axis=0)


CAND_B = 8
CAND_ROWS = PEER_TOPK + (CAND_B - 1) * CAND_B + (PEER_TOPK - CAND_B)


def _pair_rows(x1, x2, combine):
    rows = [combine(x1[0:1], x2)]
    rows += [combine(x1[a:a + 1], x2[0:CAND_B]) for a in range(1, CAND_B)]
    rows.append(combine(x1[CAND_B:], x2[0:1]))
    return jnp.concatenate(rows, axis=0)


def _peer_topk_kernel(q_ref, k1_ref, k2_ref, eidx_ref, gate_ref):
    nt = (((1,), (1,)), ((), ()))
    tn = q_ref.shape[0]
    key_id = lax.broadcasted_iota(jnp.int32, (PEER_KEYS, tn), 0).astype(F32)
    r = lax.broadcasted_iota(jnp.int32, (CAND_ROWS, tn), 0)
    mid = r - PEER_TOPK
    last = r - (CAND_ROWS - (PEER_TOPK - CAND_B))
    flat_id = jnp.where(r < PEER_TOPK, r,
                        jnp.where(last < 0, ((mid >> 3) + 1) * PEER_TOPK + (mid & (CAND_B - 1)),
                                  (last + CAND_B) * PEER_TOPK)).astype(F32)
    assert CAND_B == 8
    e_rows, g_rows = [], []
    for h in range(PEER_HEADS):
        qa = q_ref[:, (2 * h) * PEER_HALF:(2 * h + 1) * PEER_HALF].astype(BF16)
        qb = q_ref[:, (2 * h + 1) * PEER_HALF:(2 * h + 2) * PEER_HALF].astype(BF16)
        s1 = lax.dot_general(k1_ref[h].astype(BF16), qa, nt, preferred_element_type=F32)
        s2 = lax.dot_general(k2_ref[h].astype(BF16), qb, nt, preferred_element_type=F32)
        t1, i1 = _top16_rows(s1, key_id, float(PEER_KEYS))
        t2, i2 = _top16_rows(s2, key_id, float(PEER_KEYS))
        cand = _pair_rows(t1, t2, lambda a, b: a + b)
        cidx = _pair_rows(i1, i2, lambda a, b: a * float(PEER_KEYS) + b)
        top, sel = _top16_rows(cand, flat_id, float(PEER_TOPK * PEER_TOPK))
        eidx = [jnp.sum(jnp.where(flat_id == sel[r:r + 1], cidx, 0.0), axis=0, keepdims=True)
                for r in range(PEER_TOPK)]
        p = jnp.exp(top - top[0:1])
        g_rows.append(p / jnp.sum(p, axis=0, keepdims=True))
        e_rows.append(jnp.concatenate(eidx, axis=0))
    eidx_ref[...] = jnp.concatenate(e_rows, axis=0).T.astype(jnp.int32)
    gate_ref[...] = jnp.concatenate(g_rows, axis=0).T


def _peer_topk(q, keys1, keys2, *, tn):
    n, d = q.shape
    return pl.pallas_call(
        _peer_topk_kernel,
        grid=(n // tn,),
        in_specs=[
            pl.BlockSpec((tn, d), lambda i: (i, 0)),
            pl.BlockSpec(keys1.shape, lambda i: (0, 0, 0)),
            pl.BlockSpec(keys2.shape, lambda i: (0, 0, 0)),
        ],
        out_specs=[pl.BlockSpec((tn, N_SEL), lambda i: (i, 0)), pl.BlockSpec((tn, N_SEL), lambda i: (i, 0))],
        out_shape=[jax.ShapeDtypeStruct((n, N_SEL), jnp.int32), jax.ShapeDtypeStruct((n, N_SEL), F32)],
        compiler_params=_params("parallel"),
        name="peer_topk",
    )(q, keys1, keys2)


N_SLOTS = 6
HI16 = -65536
ROW_SUB = 16


def _pack_experts_kernel(u_ref, v_ref, o_ref):
    ub = lax.bitcast_convert_type(u_ref[...].astype(BF16).astype(F32), jnp.int32)
    vb = lax.bitcast_convert_type(v_ref[...].astype(BF16).astype(F32), jnp.int32)
    o_ref[...] = lax.shift_right_logical(ub, 16) | (vb & HI16)


def _pack_experts(exp_u, exp_v, layer, *, tr):
    _, n, d = exp_u.shape
    spec = pl.BlockSpec((None, tr, d), lambda i: (layer, i, 0))
    return pl.pallas_call(
        _pack_experts_kernel,
        grid=(n // tr,),
        in_specs=[spec, spec],
        out_specs=pl.BlockSpec((tr, d), lambda i: (i, 0)),
        out_shape=jax.ShapeDtypeStruct((n, d), jnp.int32),
        compiler_params=_params("parallel"),
        name="pack_experts",
    )(exp_u, exp_v)


def _sublane_sums8(ps):
    sid = lax.broadcasted_iota(jnp.int32, (8, 128), 0)
    m4 = sid < 4
    m2 = (sid & 2) == 0
    m1 = (sid & 1) == 0
    b = [jnp.where(m4, ps[k], ps[k + 4]) + pltpu.roll(jnp.where(m4, ps[k + 4], ps[k]), 4, 0) for k in range(4)]
    c = [jnp.where(m2, b[k], b[k + 2]) + jnp.where(m2, pltpu.roll(b[k], 6, 0), pltpu.roll(b[k + 2], 2, 0))
         for k in range(2)]
    return jnp.where(m1, c[0], c[1]) + jnp.where(m1, pltpu.roll(c[0], 7, 0), pltpu.roll(c[1], 1, 0))


def _block_rms_scale(x):
    ss = jnp.sum(jnp.sum(x * x, axis=2, keepdims=True), axis=1, keepdims=True)
    return lax.rsqrt(ss * (1.0 / (ROW_SUB * 128)) + EPS)


def _peer_experts_kernel(eidx_ref, x_ref, gate_ref, g_ref, gf_ref, uv_hbm, o_ref, buf, sem, ga_ref, h_ref,
                         *, tb, final_norm):
    n_groups = N_SEL // 8
    per = N_SEL // (2 * n_groups)
    lookahead = N_SLOTS - 1

    def issue(t, lo, hi):
        slot = lax.rem(t, N_SLOTS)
        for m in range(lo, hi):
            e = eidx_ref[t, m]
            pltpu.make_async_copy(uv_hbm.at[pl.ds(e, 1)], buf.at[slot, pl.ds(m, 1)],
                                  sem.at[slot]).start(priority=m % 2)

    def wait(t):
        slot = lax.rem(t, N_SLOTS)
        pltpu.make_async_copy(uv_hbm.at[pl.ds(0, N_SEL)], buf.at[slot], sem.at[slot]).wait()

    eye = (lax.broadcasted_iota(jnp.int32, (N_SEL, N_SEL), 0)
           == lax.broadcasted_iota(jnp.int32, (N_SEL, N_SEL), 1))

    def stage_a_groups(t, issue_next):
        slot = lax.rem(t, N_SLOTS)
        h0, h1 = h_ref[t, 0:8, :], h_ref[t, 8:16, :]
        qs = []
        for gi in range(n_groups):
            issue_next(gi * per, (gi + 1) * per)
            ps = []
            for j in range(8):
                m = gi * 8 + j
                u0 = lax.bitcast_convert_type(lax.shift_left(buf[slot, m, 0:8, :], 16), F32)
                u1 = lax.bitcast_convert_type(lax.shift_left(buf[slot, m, 8:16, :], 16), F32)
                ps.append(u0 * h0 + u1 * h1)
            qs.append(_sublane_sums8(ps))
        return qs

    def stage_a_tail(t, qs):
        act = _gelu_tanh(jnp.sum(jnp.concatenate(qs, axis=0), axis=-1, keepdims=True))
        gate_col = jnp.sum(jnp.where(eye, gate_ref[pl.ds(t, 1), :], 0.0), axis=-1, keepdims=True)
        ga_ref[lax.rem(t, 2)] = jnp.broadcast_to(gate_col * act, (N_SEL, 128))

    def stage_b(t, issue_next):
        slot = lax.rem(t, N_SLOTS)
        par = lax.rem(t, 2)
        acc0 = jnp.zeros((8, 128), F32)
        acc1 = jnp.zeros((8, 128), F32)
        for m in range(N_SEL):
            if m % 8 == 0:
                gi = m // 8
                issue_next(N_SEL // 2 + gi * per, N_SEL // 2 + (gi + 1) * per)
            gm = jnp.broadcast_to(ga_ref[par, pl.ds(m, 1), :], (8, 128))
            acc0 = acc0 + gm * lax.bitcast_convert_type(buf[slot, m, 0:8, :] & HI16, F32)
            acc1 = acc1 + gm * lax.bitcast_convert_type(buf[slot, m, 8:16, :] & HI16, F32)
        o_ref[t] = x_ref[t] + jnp.concatenate([acc0, acc1], axis=0)

    def no_issue(lo, hi):
        del lo, hi

    for t in range(lookahead):
        issue(t, 0, N_SEL)
    xb = x_ref[...]
    h_ref[...] = xb * _block_rms_scale(xb) * g_ref[...]
    wait(0)
    stage_a_tail(0, stage_a_groups(0, no_issue))

    def steady(t, carry):
        wait(t + 1)
        issue_next = lambda lo, hi: issue(t + lookahead, lo, hi)
        qs = stage_a_groups(t + 1, issue_next)
        stage_b(t, issue_next)
        stage_a_tail(t + 1, qs)
        return carry

    def drain(t, carry):
        wait(t + 1)
        qs = stage_a_groups(t + 1, no_issue)
        stage_b(t, no_issue)
        stage_a_tail(t + 1, qs)
        return carry

    lax.fori_loop(0, tb - lookahead, steady, 0)
    lax.fori_loop(tb - lookahead, tb - 1, drain, 0)
    stage_b(tb - 1, no_issue)
    if final_norm:
        yb = o_ref[...]
        o_ref[...] = yb * _block_rms_scale(yb) * gf_ref[...]


def _peer_experts(x, eidx, gate, g_ffn, g_final, exp_uv, *, tb, final_norm):
    n, d = x.shape
    slab = (ROW_SUB, 128)
    xspec = pl.BlockSpec((tb,) + slab, lambda i: (i, 0, 0))
    gspec = pl.BlockSpec(slab, lambda i: (0, 0))
    kern = functools.partial(_peer_experts_kernel, tb=tb, final_norm=final_norm)
    out = pl.pallas_call(
        kern,
        grid=(n // tb,),
        in_specs=[
            pl.BlockSpec((tb, N_SEL), lambda i: (i, 0), memory_space=pltpu.SMEM),
            xspec,
            pl.BlockSpec((tb, N_SEL), lambda i: (i, 0)),
            gspec,
            gspec,
            pl.BlockSpec(memory_space=pl.ANY),
        ],
        out_specs=xspec,
        out_shape=jax.ShapeDtypeStruct((n,) + slab, F32),
        scratch_shapes=[
            pltpu.VMEM((N_SLOTS, N_SEL) + slab, jnp.int32),
            pltpu.SemaphoreType.DMA((N_SLOTS,)),
            pltpu.VMEM((2, N_SEL, 128), F32),
            pltpu.VMEM((tb,) + slab, F32),
        ],
        compiler_params=_params("arbitrary"),
        name="peer_experts",
    )(eidx, x.reshape((n,) + slab), gate, g_ffn.reshape(slab), g_final.reshape(slab), exp_uv)
    return out.reshape(n, d)


def _layer(x3, k_past, v_past, sc_past, cc_past, w, g_final, final_norm):
    (g_mix, w_in16, w_sconv, w_cconv, b_cconv, g_cnorm, b_cnorm, g_vnorm, b_vnorm,
     w_spatial, b_spatial, w_out, g_ffn, w_query16, keys1, keys2, exp_uv) = w
    b, t, d = x3.shape
    n = b * t
    x = x3.reshape(n, d)
    tm = min(512, n)

    proj = _norm_matmul(x, g_mix, w_in16, tm=tm, tn=1280)
    proj3 = proj.reshape(b, t, proj.shape[1])
    k_new = proj3[:, :, G_WIDTH:2 * G_WIDTH]
    v_new = proj3[:, :, 2 * G_WIDTH:3 * G_WIDTH]
    if k_past is None:
        y_a = _sb_attention(proj3, 0, proj3, 1, proj3, 2, t_q=t, tq=128, tk=128, q_offset=0)
    else:
        past = k_past.shape[1]
        k_all = jnp.concatenate([k_past.reshape(b, past, G_WIDTH), k_new], axis=1)
        v_all = jnp.concatenate([v_past.reshape(b, past, G_WIDTH), v_new], axis=1)
        y_a = _sb_attention(proj3, 0, k_all, 0, v_all, 0, t_q=t, tq=t, tk=t, q_offset=past)
    y_bcd, vn, sc_state, cc_state = _mixers(
        proj3, sc_past, cc_past, w_sconv, w_cconv, b_cconv, g_cnorm, b_cnorm, g_vnorm, b_vnorm,
        w_spatial, b_spatial, tt=min(512, t))
    x1 = _out_proj(x, y_a.reshape(n, G_WIDTH), y_bcd.reshape(n, 3 * G_WIDTH), w_out, tm=min(256, n))
    q = _norm_matmul(x1, g_ffn, w_query16, tm=tm, tn=1024)
    eidx, gate = _peer_topk(q, keys1, keys2, tn=min(512, n))
    x2 = _peer_experts(x1, eidx, gate, g_ffn, g_final, exp_uv, tb=min(128, n), final_norm=final_norm)
    heads = (b, t, N_HEADS, HEAD_DIM)
    return x2.reshape(b, t, d), (k_new.reshape(heads), v_new.reshape(heads), sc_state, cc_state, vn)


def kernel(x_prompt, x_sample, cache_k, cache_v, state_sconv, state_cconv, g_mix, w_in, w_sconv, w_cconv,
           b_cconv, g_cnorm, b_cnorm, g_vnorm, b_vnorm, w_spatial, b_spatial, w_out, g_ffn, w_query,
           sub_keys1, sub_keys2, expert_u, expert_v, g_final):
    depth = w_in.shape[0]
    xp, xs = x_prompt, x_sample
    nb = xp.shape[0]
    outs = [[] for _ in range(9)]
    for l in range(depth):
        w = (g_mix[l], w_in[l].astype(BF16), w_sconv[l], w_cconv[l], b_cconv[l], g_cnorm[l], b_cnorm[l],
             g_vnorm[l], b_vnorm[l], w_spatial[l], b_spatial[l], w_out[l], g_ffn[l],
             w_query[l].astype(BF16), sub_keys1[l], sub_keys2[l],
             _pack_experts(expert_u, expert_v, l, tr=256).reshape(expert_u.shape[1], ROW_SUB, 128))
        last = l == depth - 1
        sc0 = jnp.zeros((nb, SCONV_W - 1, G_WIDTH), F32)
        cc0 = jnp.zeros((nb, CCONV_W - 1, G_WIDTH), F32)
        xp, (k1, v1, sc1, cc1, _) = _layer(xp, None, None, sc0, cc0, w, g_final, last)
        xs, (k2, v2, sc2, cc2, gv2) = _layer(xs, cache_k[l], cache_v[l], state_sconv[l], state_cconv[l],
                                             w, g_final, last)
        for lst, val in zip(outs, (k1, v1, k2, v2, sc1, sc2, cc1, cc2, gv2)):
            lst.append(val)
    kp, vp, ksm, vsm, scp, scs, ccp, ccs, gvs = [jnp.stack(o) for o in outs]
    return (xp, xs, kp, vp, ksm, vsm, scp, scs, ccp, ccs, gvs)
```

```python
import functools

import jax
import jax.numpy as jnp
from jax import lax
from jax.experimental import pallas as pl
from jax.experimental.pallas import tpu as pltpu

F32 = jnp.float32
BF16 = jnp.bfloat16
EPS = 1e-6

G_WIDTH = 512
N_HEADS = 8
HEAD_DIM = 64
SCONV_W = 3
CCONV_W = 31
MLP_CHUNK = 128
PEER_HEADS = 8
PEER_KEYS = 128
PEER_TOPK = 16
PEER_HALF = 128
N_SEL = PEER_HEADS * PEER_TOPK

VMEM_LIMIT_BYTES = 56 * 1024 * 1024


def _params(*sem):
    return pltpu.CompilerParams(dimension_semantics=sem, vmem_limit_bytes=VMEM_LIMIT_BYTES)


def _gelu_tanh(x):
    return 0.5 * x * (1.0 + jnp.tanh(0.7978845608028654 * (x + 0.044715 * (x * x * x))))


def _layernorm(x, g, b):
    xc = x - jnp.mean(x, axis=-1, keepdims=True)
    var = jnp.mean(xc * xc, axis=-1, keepdims=True)
    return xc * lax.rsqrt(var + EPS) * g + b


def _rmsnorm_rows(x, g):
    return x * lax.rsqrt(jnp.mean(x * x, axis=-1, keepdims=True) + EPS) * g


def _norm_matmul_kernel(x_ref, g_ref, w_ref, o_ref, h_ref):
    @pl.when(pl.program_id(1) == 0)
    def _():
        h_ref[...] = _rmsnorm_rows(x_ref[...], g_ref[...]).astype(BF16)

    o_ref[...] = jnp.dot(h_ref[...], w_ref[...], preferred_element_type=F32)


def _norm_matmul(x, g, w, *, tm, tn):
    n, d = x.shape
    nout = w.shape[1]
    return pl.pallas_call(
        _norm_matmul_kernel,
        grid=(n // tm, nout // tn),
        in_specs=[
            pl.BlockSpec((tm, d), lambda i, j: (i, 0)),
            pl.BlockSpec((1, d), lambda i, j: (0, 0)),
            pl.BlockSpec((d, tn), lambda i, j: (0, j)),
        ],
        out_specs=pl.BlockSpec((tm, tn), lambda i, j: (i, j)),
        out_shape=jax.ShapeDtypeStruct((n, nout), F32),
        scratch_shapes=[pltpu.VMEM((tm, d), BF16)],
        compiler_params=_params("parallel", "arbitrary"),
        name="norm_matmul",
    )(x, g.reshape(1, d), w)


RUN_CUTOFF = -104.0


def _sb_kernel(q_ref, k_ref, v_ref, o_ref, run_ref, acc_ref, q16_ref, *, tq, tk, q_offset):
    qi = pl.program_id(1)
    q_pos0 = q_offset + qi * tq
    n_kb = jnp.minimum(pl.cdiv(q_pos0 + tq, tk), k_ref.shape[1] // tk)

    run_ref[...] = jnp.zeros_like(run_ref)
    acc_ref[...] = jnp.zeros_like(acc_ref)
    for h in range(N_HEADS):
        q16_ref[h] = (q_ref[0, :, h * HEAD_DIM:(h + 1) * HEAD_DIM] * (HEAD_DIM ** -0.5)).astype(BF16)

    rows = N_HEADS * tq
    row = lax.broadcasted_iota(jnp.int32, (rows, tk), 0) & (tq - 1)
    col_minus_row = lax.broadcasted_iota(jnp.int32, (rows, tk), 1) - row
    col = lax.broadcasted_iota(jnp.int32, (2 * tk, 2 * tk), 1)
    krow = lax.broadcasted_iota(jnp.int32, (2 * tk, 2 * tk), 0) & (tk - 1)
    suffix_and_total = jnp.where((krow >= col) | (col >= tk), 1.0, 0.0).astype(BF16)

    def body(carry):
        it, _ = carry
        kb = n_kb - 1 - it
        k0 = pl.multiple_of(kb * tk, tk)
        mask = col_minus_row < (q_pos0 - k0)
        zs = []
        for h in range(N_HEADS):
            kh = k_ref[0, pl.ds(k0, tk), h * HEAD_DIM:(h + 1) * HEAD_DIM].astype(BF16)
            zs.append(lax.dot_general(q16_ref[h], kh, (((1,), (1,)), ((), ())), preferred_element_type=F32))
        z = jnp.concatenate(zs, axis=0)
        lf = -(jnp.maximum(z, 0.0) + jnp.log1p(jnp.exp(-jnp.abs(z))))
        lf = jnp.where(mask, lf, 0.0)
        lf_hi = lf.astype(BF16)
        lf_lo = (lf - lf_hi.astype(F32)).astype(BF16)
        sums = jnp.dot(jnp.concatenate([lf_hi, lf_lo], axis=1), suffix_and_total, preferred_element_type=F32)
        run = run_ref[...]
        w = jnp.where(mask, jnp.exp(z + sums[:, :tk] + run), 0.0).astype(BF16)
        for h in range(N_HEADS):
            vh = v_ref[0, pl.ds(k0, tk), h * HEAD_DIM:(h + 1) * HEAD_DIM].astype(BF16)
            acc_ref[h] += jnp.dot(w[h * tq:(h + 1) * tq], vh, preferred_element_type=F32)
        run = run + sums[:, tk:]
        run_ref[...] = run
        return it + 1, jnp.max(run)

    lax.while_loop(lambda c: (c[0] < n_kb) & (c[1] > RUN_CUTOFF), body, (0, 0.0))
    o_ref[0] = jnp.concatenate([acc_ref[h] for h in range(N_HEADS)], axis=-1)


def _sb_attention(q_arr, q_col, k_arr, k_col, v_arr, v_col, *, t_q, tq, tk, q_offset):
    b = q_arr.shape[0]
    t_k = k_arr.shape[1]
    kern = functools.partial(_sb_kernel, tq=tq, tk=tk, q_offset=q_offset)
    return pl.pallas_call(
        kern,
        grid=(b, t_q // tq),
        in_specs=[
            pl.BlockSpec((1, tq, G_WIDTH), lambda i, j: (i, j, q_col)),
            pl.BlockSpec((1, t_k, G_WIDTH), lambda i, j: (i, 0, k_col)),
            pl.BlockSpec((1, t_k, G_WIDTH), lambda i, j: (i, 0, v_col)),
        ],
        out_specs=pl.BlockSpec((1, tq, G_WIDTH), lambda i, j: (i, j, 0)),
        out_shape=jax.ShapeDtypeStruct((b, t_q, G_WIDTH), F32),
        scratch_shapes=[
            pltpu.VMEM((N_HEADS * tq, tk), F32),
            pltpu.VMEM((N_HEADS, tq, HEAD_DIM), F32),
            pltpu.VMEM((N_HEADS, tq, HEAD_DIM), BF16),
        ],
        compiler_params=_params("parallel", "arbitrary"),
        name="sb_attention",
    )(q_arr, k_arr, v_arr)


SC_HIST = 8
CC_HIST = 32


def _mixers_kernel(scb_ref, scc_ref, sch_ref, cfa_ref, cfg_ref, gmu_ref, gmv_ref,
                   sc_past_ref, cc_past_ref, wsc_ref, wcc_ref, bcc_ref, gcn_ref, bcn_ref,
                   gvn_ref, bvn_ref, wsp_ref, bsp_ref,
                   y_ref, vn_ref, sc_state_ref, cc_state_ref, u_buf, g_buf, *, tt, chunk):
    ti = pl.program_id(1)

    @pl.when(ti == 0)
    def _():
        u_buf[0:SC_HIST, :] = jnp.zeros((SC_HIST, G_WIDTH), F32)
        g_buf[0:CC_HIST, :] = jnp.zeros((CC_HIST, G_WIDTH), F32)
        u_buf[SC_HIST - (SCONV_W - 1):SC_HIST, :] = sc_past_ref[0]
        g_buf[CC_HIST - (CCONV_W - 1):CC_HIST, :] = cc_past_ref[0]

    u_buf[SC_HIST:SC_HIST + tt, :] = scc_ref[0] * sch_ref[0]
    conv_b = jnp.zeros((tt, G_WIDTH), F32)
    for k in range(SCONV_W):
        s0 = SC_HIST - (SCONV_W - 1) + k
        conv_b = conv_b + wsc_ref[k:k + 1, :] * u_buf[s0:s0 + tt, :]
    y_ref[0, :, 0:G_WIDTH] = scb_ref[0] * conv_b

    g_buf[CC_HIST:CC_HIST + tt, :] = cfa_ref[0] * jax.nn.sigmoid(cfg_ref[0])
    conv_c = jnp.zeros((tt, G_WIDTH), F32)
    for k in range(CCONV_W):
        s0 = CC_HIST - (CCONV_W - 1) + k
        conv_c = conv_c + wcc_ref[k:k + 1, :] * g_buf[s0:s0 + tt, :]
    ln = _layernorm(conv_c + bcc_ref[...], gcn_ref[...], bcn_ref[...])
    y_ref[0, :, G_WIDTH:2 * G_WIDTH] = ln * jax.nn.sigmoid(ln)

    vn = _layernorm(_gelu_tanh(gmv_ref[0]), gvn_ref[...], bvn_ref[...])
    vn_ref[0] = vn
    vn16 = vn.astype(BF16)
    for c in range(tt // chunk):
        rows = slice(c * chunk, (c + 1) * chunk)
        parts = []
        for h in range(N_HEADS):
            hs = slice(h * HEAD_DIM, (h + 1) * HEAD_DIM)
            parts.append(jnp.dot(wsp_ref[h], vn16[rows, hs], preferred_element_type=F32))
        s = jnp.concatenate(parts, axis=-1) + bsp_ref[...]
        y_ref[0, rows, 2 * G_WIDTH:3 * G_WIDTH] = _gelu_tanh(gmu_ref[0, rows, :]) * s

    sc_state_ref[0] = u_buf[SC_HIST + tt - (SCONV_W - 1):SC_HIST + tt, :]
    cc_state_ref[0] = g_buf[CC_HIST + tt - (CCONV_W - 1):CC_HIST + tt, :]
    u_buf[0:SC_HIST, :] = u_buf[tt:tt + SC_HIST, :]
    g_buf[0:CC_HIST, :] = g_buf[tt:tt + CC_HIST, :]


def _mixers(proj3, sc_past, cc_past, w_sconv, w_cconv, b_cconv, g_cnorm, b_cnorm, g_vnorm, b_vnorm,
            w_spatial, b_spatial, *, tt):
    b, t, _ = proj3.shape
    chunk = min(t, MLP_CHUNK)
    wsp = jnp.tril(w_spatial[:, :chunk, :chunk]).astype(BF16)
    bsp = jnp.repeat(b_spatial[:, :chunk].T, HEAD_DIM, axis=1)

    def col(c):
        return pl.BlockSpec((1, tt, G_WIDTH), lambda i, j: (i, j, c))

    def whole(a):
        return pl.BlockSpec(a.shape, lambda i, j: (0,) * a.ndim)

    def per_batch(a):
        return pl.BlockSpec((1,) + a.shape[1:], lambda i, j: (i,) + (0,) * (a.ndim - 1))

    row = lambda a: a.reshape(1, G_WIDTH)
    small = [w_sconv, w_cconv, row(b_cconv), row(g_cnorm), row(b_cnorm), row(g_vnorm), row(b_vnorm), wsp, bsp]
    kern = functools.partial(_mixers_kernel, tt=tt, chunk=chunk)
    return pl.pallas_call(
        kern,
        grid=(b, t // tt),
        in_specs=[col(c) for c in range(3, 10)] + [per_batch(sc_past), per_batch(cc_past)]
                 + [whole(a) for a in small],
        out_specs=[
            pl.BlockSpec((1, tt, 3 * G_WIDTH), lambda i, j: (i, j, 0)),
            pl.BlockSpec((1, tt, G_WIDTH), lambda i, j: (i, j, 0)),
            pl.BlockSpec((1, SCONV_W - 1, G_WIDTH), lambda i, j: (i, 0, 0)),
            pl.BlockSpec((1, CCONV_W - 1, G_WIDTH), lambda i, j: (i, 0, 0)),
        ],
        out_shape=[
            jax.ShapeDtypeStruct((b, t, 3 * G_WIDTH), F32),
            jax.ShapeDtypeStruct((b, t, G_WIDTH), F32),
            jax.ShapeDtypeStruct((b, SCONV_W - 1, G_WIDTH), F32),
            jax.ShapeDtypeStruct((b, CCONV_W - 1, G_WIDTH), F32),
        ],
        scratch_shapes=[pltpu.VMEM((SC_HIST + tt, G_WIDTH), F32), pltpu.VMEM((CC_HIST + tt, G_WIDTH), F32)],
        compiler_params=_params("parallel", "arbitrary"),
        name="mixers",
    )(*([proj3] * 7), sc_past, cc_past, *small)


def _out_proj_kernel(x_ref, ya_ref, yb_ref, wa_ref, wb_ref, o_ref):
    o_ref[...] = (x_ref[...]
                  + jnp.dot(ya_ref[...].astype(BF16), wa_ref[...], preferred_element_type=F32)
                  + jnp.dot(yb_ref[...].astype(BF16), wb_ref[...], preferred_element_type=F32))


def _out_proj(x, ya, yb, w_out, *, tm):
    n, d = x.shape
    wa = w_out[:G_WIDTH].astype(BF16)
    wb = w_out[G_WIDTH:].astype(BF16)
    return pl.pallas_call(
        _out_proj_kernel,
        grid=(n // tm,),
        in_specs=[
            pl.BlockSpec((tm, d), lambda i: (i, 0)),
            pl.BlockSpec((tm, G_WIDTH), lambda i: (i, 0)),
            pl.BlockSpec((tm, 3 * G_WIDTH), lambda i: (i, 0)),
            pl.BlockSpec(wa.shape, lambda i: (0, 0)),
            pl.BlockSpec(wb.shape, lambda i: (0, 0)),
        ],
        out_specs=pl.BlockSpec((tm, d), lambda i: (i, 0)),
        out_shape=jax.ShapeDtypeStruct((n, d), F32),
        compiler_params=_params("parallel"),
        name="out_proj",
    )(x, ya, yb, wa, wb)


def _top16_rows(s, rid, rid_end):
    vals, idxs = [], []
    for _ in range(PEER_TOPK):
        m = jnp.max(s, axis=0, keepdims=True)
        i = jnp.min(jnp.where(s == m, rid, rid_end), axis=0, keepdims=True)
        s = jnp.where(rid == i, -jnp.inf, s)
        vals.append(m)
        idxs.append(i)
    return jnp.concatenate(vals, axis=0), jnp.concatenate(idxs, axis=0)


CAND_B = 8
CAND_ROWS = PEER_TOPK + (CAND_B - 1) * CAND_B + (PEER_TOPK - CAND_B)


def _pair_rows(x1, x2, combine):
    rows = [combine(x1[0:1], x2)]
    rows += [combine(x1[a:a + 1], x2[0:CAND_B]) for a in range(1, CAND_B)]
    rows.append(combine(x1[CAND_B:], x2[0:1]))
    return jnp.concatenate(rows, axis=0)


def _peer_topk_kernel(q_ref, k1_ref, k2_ref, eidx_ref, gate_ref):
    nt = (((1,), (1,)), ((), ()))
    tn = q_ref.shape[0]
    key_id = lax.broadcasted_iota(jnp.int32, (PEER_KEYS, tn), 0).astype(F32)
    r = lax.broadcasted_iota(jnp.int32, (CAND_ROWS, tn), 0)
    mid = r - PEER_TOPK
    last = r - (CAND_ROWS - (PEER_TOPK - CAND_B))
    flat_id = jnp.where(r < PEER_TOPK, r,
                        jnp.where(last < 0, ((mid >> 3) + 1) * PEER_TOPK + (mid & (CAND_B - 1)),
                                  (last + CAND_B) * PEER_TOPK)).astype(F32)
    assert CAND_B == 8
    e_rows, g_rows = [], []
    for h in range(PEER_HEADS):
        qa = q_ref[:, (2 * h) * PEER_HALF:(2 * h + 1) * PEER_HALF].astype(BF16)
        qb = q_ref[:, (2 * h + 1) * PEER_HALF:(2 * h + 2) * PEER_HALF].astype(BF16)
        s1 = lax.dot_general(k1_ref[h].astype(BF16), qa, nt, preferred_element_type=F32)
        s2 = lax.dot_general(k2_ref[h].astype(BF16), qb, nt, preferred_element_type=F32)
        t1, i1 = _top16_rows(s1, key_id, float(PEER_KEYS))
        t2, i2 = _top16_rows(s2, key_id, float(PEER_KEYS))
        cand = _pair_rows(t1, t2, lambda a, b: a + b)
        cidx = _pair_rows(i1, i2, lambda a, b: a * float(PEER_KEYS) + b)
        top, sel = _top16_rows(cand, flat_id, float(PEER_TOPK * PEER_TOPK))
        eidx = [jnp.sum(jnp.where(flat_id == sel[r:r + 1], cidx, 0.0), axis=0, keepdims=True)
                for r in range(PEER_TOPK)]
        p = jnp.exp(top - top[0:1])
        g_rows.append(p / jnp.sum(p, axis=0, keepdims=True))
        e_rows.append(jnp.concatenate(eidx, axis=0))
    eidx_ref[...] = jnp.concatenate(e_rows, axis=0).T.astype(jnp.int32)
    gate_ref[...] = jnp.concatenate(g_rows, axis=0).T


def _peer_topk(q, keys1, keys2, *, tn):
    n, d = q.shape
    return pl.pallas_call(
        _peer_topk_kernel,
        grid=(n // tn,),
        in_specs=[
            pl.BlockSpec((tn, d), lambda i: (i, 0)),
            pl.BlockSpec(keys1.shape, lambda i: (0, 0, 0)),
            pl.BlockSpec(keys2.shape, lambda i: (0, 0, 0)),
        ],
        out_specs=[pl.BlockSpec((tn, N_SEL), lambda i: (i, 0)), pl.BlockSpec((tn, N_SEL), lambda i: (i, 0))],
        out_shape=[jax.ShapeDtypeStruct((n, N_SEL), jnp.int32), jax.ShapeDtypeStruct((n, N_SEL), F32)],
        compiler_params=_params("parallel"),
        name="peer_topk",
    )(q, keys1, keys2)


N_SLOTS = 8
HI16 = -65536
ROW_SUB = 16


def _pack_experts_kernel(u_ref, v_ref, o_ref):
    ub = lax.bitcast_convert_type(u_ref[...].astype(BF16).astype(F32), jnp.int32)
    vb = lax.bitcast_convert_type(v_ref[...].astype(BF16).astype(F32), jnp.int32)
    o_ref[...] = lax.shift_right_logical(ub, 16) | (vb & HI16)


def _pack_experts(exp_u, exp_v, layer, *, tr):
    _, n, d = exp_u.shape
    spec = pl.BlockSpec((None, tr, d), lambda i: (layer, i, 0))
    return pl.pallas_call(
        _pack_experts_kernel,
        grid=(n // tr,),
        in_specs=[spec, spec],
        out_specs=pl.BlockSpec((tr, d), lambda i: (i, 0)),
        out_shape=jax.ShapeDtypeStruct((n, d), jnp.int32),
        compiler_params=_params("parallel"),
        name="pack_experts",
    )(exp_u, exp_v)


def _sublane_sums8(ps):
    sid = lax.broadcasted_iota(jnp.int32, (8, 128), 0)
    m4 = sid < 4
    m2 = (sid & 2) == 0
    m1 = (sid & 1) == 0
    b = [jnp.where(m4, ps[k], ps[k + 4]) + pltpu.roll(jnp.where(m4, ps[k + 4], ps[k]), 4, 0) for k in range(4)]
    c = [jnp.where(m2, b[k], b[k + 2]) + jnp.where(m2, pltpu.roll(b[k], 6, 0), pltpu.roll(b[k + 2], 2, 0))
         for k in range(2)]
    return jnp.where(m1, c[0], c[1]) + jnp.where(m1, pltpu.roll(c[0], 7, 0), pltpu.roll(c[1], 1, 0))


def _block_rms_scale(x):
    ss = jnp.sum(jnp.sum(x * x, axis=2, keepdims=True), axis=1, keepdims=True)
    return lax.rsqrt(ss * (1.0 / (ROW_SUB * 128)) + EPS)


def _peer_experts_kernel(eidx_ref, x_ref, gate_ref, g_ref, gf_ref, uv_hbm, o_ref, buf, sem, ga_ref, h_ref,
                         *, tb, final_norm):
    n_groups = N_SEL // 8
    per = N_SEL // (2 * n_groups)
    lookahead = N_SLOTS - 1

    def issue(t, slot, lo, hi):
        for m in range(lo, hi):
            e = eidx_ref[t, m]
            pltpu.make_async_copy(uv_hbm.at[pl.ds(e, 1)], buf.at[slot, pl.ds(m, 1)],
                                  sem.at[slot]).start(priority=m % 2)

    def wait(slot):
        pltpu.make_async_copy(uv_hbm.at[pl.ds(0, N_SEL)], buf.at[slot], sem.at[slot]).wait()

    eye = (lax.broadcasted_iota(jnp.int32, (N_SEL, N_SEL), 0)
           == lax.broadcasted_iota(jnp.int32, (N_SEL, N_SEL), 1))

    def stage_a_groups(t, slot, issue_next):
        h0, h1 = h_ref[t, 0:8, :], h_ref[t, 8:16, :]
        qs = []
        for gi in range(n_groups):
            issue_next(gi * per, (gi + 1) * per)
            ps = []
            for j in range(8):
                m = gi * 8 + j
                u0 = lax.bitcast_convert_type(lax.shift_left(buf[slot, m, 0:8, :], 16), F32)
                u1 = lax.bitcast_convert_type(lax.shift_left(buf[slot, m, 8:16, :], 16), F32)
                ps.append(u0 * h0 + u1 * h1)
            qs.append(_sublane_sums8(ps))
        return qs

    def stage_a_tail(t, par, qs):
        act = _gelu_tanh(jnp.sum(jnp.concatenate(qs, axis=0), axis=-1, keepdims=True))
        gate_col = jnp.sum(jnp.where(eye, gate_ref[pl.ds(t, 1), :], 0.0), axis=-1, keepdims=True)
        ga_ref[par] = jnp.broadcast_to(gate_col * act, (N_SEL, 128))

    def stage_b(t, slot, par, issue_next):
        acc0 = jnp.zeros((8, 128), F32)
        acc1 = jnp.zeros((8, 128), F32)
        for m in range(N_SEL):
            if m % 8 == 0:
                gi = m // 8
                issue_next(N_SEL // 2 + gi * per, N_SEL // 2 + (gi + 1) * per)
            gm = jnp.broadcast_to(ga_ref[par, pl.ds(m, 1), :], (8, 128))
            acc0 = acc0 + gm * lax.bitcast_convert_type(buf[slot, m, 0:8, :] & HI16, F32)
            acc1 = acc1 + gm * lax.bitcast_convert_type(buf[slot, m, 8:16, :] & HI16, F32)
        o_ref[t] = x_ref[t] + jnp.concatenate([acc0, acc1], axis=0)

    def no_issue(lo, hi):
        del lo, hi

    def token(t, j, with_issue):
        issue_next = (lambda lo, hi: issue(t + lookahead, (j + lookahead) % N_SLOTS, lo, hi)) if with_issue else no_issue
        wait((j + 1) % N_SLOTS)
        qs = stage_a_groups(t + 1, (j + 1) % N_SLOTS, issue_next)
        stage_b(t, j, j % 2, issue_next)
        stage_a_tail(t + 1, (j + 1) % 2, qs)

    for t in range(lookahead):
        issue(t, t, 0, N_SEL)
    xb = x_ref[...]
    h_ref[...] = xb * _block_rms_scale(xb) * g_ref[...]
    wait(0)
    stage_a_tail(0, 0, stage_a_groups(0, 0, no_issue))

    n_full = (tb - lookahead) // N_SLOTS

    def steady(i, carry):
        for j in range(N_SLOTS):
            token(i * N_SLOTS + j, j, True)
        return carry

    lax.fori_loop(0, n_full, steady, 0)
    for t in range(n_full * N_SLOTS, tb - 1):
        token(t, t % N_SLOTS, t + lookahead < tb)
    stage_b(tb - 1, (tb - 1) % N_SLOTS, (tb - 1) % 2, no_issue)
    if final_norm:
        yb = o_ref[...]
        o_ref[...] = yb * _block_rms_scale(yb) * gf_ref[...]


def _peer_experts(x, eidx, gate, g_ffn, g_final, exp_uv, *, tb, final_norm):
    n, d = x.shape
    slab = (ROW_SUB, 128)
    xspec = pl.BlockSpec((tb,) + slab, lambda i: (i, 0, 0))
    gspec = pl.BlockSpec(slab, lambda i: (0, 0))
    kern = functools.partial(_peer_experts_kernel, tb=tb, final_norm=final_norm)
    out = pl.pallas_call(
        kern,
        grid=(n // tb,),
        in_specs=[
            pl.BlockSpec((tb, N_SEL), lambda i: (i, 0), memory_space=pltpu.SMEM),
            xspec,
            pl.BlockSpec((tb, N_SEL), lambda i: (i, 0)),
            gspec,
            gspec,
            pl.BlockSpec(memory_space=pl.ANY),
        ],
        out_specs=xspec,
        out_shape=jax.ShapeDtypeStruct((n,) + slab, F32),
        scratch_shapes=[
            pltpu.VMEM((N_SLOTS, N_SEL) + slab, jnp.int32),
            pltpu.SemaphoreType.DMA((N_SLOTS,)),
            pltpu.VMEM((2, N_SEL, 128), F32),
            pltpu.VMEM((tb,) + slab, F32),
        ],
        compiler_params=_params("arbitrary"),
        name="peer_experts",
    )(eidx, x.reshape((n,) + slab), gate, g_ffn.reshape(slab), g_final.reshape(slab), exp_uv)
    return out.reshape(n, d)


def _layer(x3, k_past, v_past, sc_past, cc_past, w, g_final, final_norm):
    (g_mix, w_in16, w_sconv, w_cconv, b_cconv, g_cnorm, b_cnorm, g_vnorm, b_vnorm,
     w_spatial, b_spatial, w_out, g_ffn, w_query16, keys1, keys2, exp_uv) = w
    b, t, d = x3.shape
    n = b * t
    x = x3.reshape(n, d)
    tm = min(512, n)

    proj = _norm_matmul(x, g_mix, w_in16, tm=tm, tn=1280)
    proj3 = proj.reshape(b, t, proj.shape[1])
    k_new = proj3[:, :, G_WIDTH:2 * G_WIDTH]
    v_new = proj3[:, :, 2 * G_WIDTH:3 * G_WIDTH]
    if k_past is None:
        y_a = _sb_attention(proj3, 0, proj3, 1, proj3, 2, t_q=t, tq=128, tk=128, q_offset=0)
    else:
        past = k_past.shape[1]
        k_all = jnp.concatenate([k_past.reshape(b, past, G_WIDTH), k_new], axis=1)
        v_all = jnp.concatenate([v_past.reshape(b, past, G_WIDTH), v_new], axis=1)
        y_a = _sb_attention(proj3, 0, k_all, 0, v_all, 0, t_q=t, tq=t, tk=t, q_offset=past)
    y_bcd, vn, sc_state, cc_state = _mixers(
        proj3, sc_past, cc_past, w_sconv, w_cconv, b_cconv, g_cnorm, b_cnorm, g_vnorm, b_vnorm,
        w_spatial, b_spatial, tt=min(512, t))
    x1 = _out_proj(x, y_a.reshape(n, G_WIDTH), y_bcd.reshape(n, 3 * G_WIDTH), w_out, tm=min(256, n))
    q = _norm_matmul(x1, g_ffn, w_query16, tm=tm, tn=1024)
    eidx, gate = _peer_topk(q, keys1, keys2, tn=min(512, n))
    x2 = _peer_experts(x1, eidx, gate, g_ffn, g_final, exp_uv, tb=min(256, n), final_norm=final_norm)
    heads = (b, t, N_HEADS, HEAD_DIM)
    return x2.reshape(b, t, d), (k_new.reshape(heads), v_new.reshape(heads), sc_state, cc_state, vn)


def kernel(x_prompt, x_sample, cache_k, cache_v, state_sconv, state_cconv, g_mix, w_in, w_sconv, w_cconv,
           b_cconv, g_cnorm, b_cnorm, g_vnorm, b_vnorm, w_spatial, b_spatial, w_out, g_ffn, w_query,
           sub_keys1, sub_keys2, expert_u, expert_v, g_final):
    depth = w_in.shape[0]
    xp, xs = x_prompt, x_sample
    nb = xp.shape[0]
    outs = [[] for _ in range(9)]
    for l in range(depth):
        w = (g_mix[l], w_in[l].astype(BF16), w_sconv[l], w_cconv[l], b_cconv[l], g_cnorm[l], b_cnorm[l],
             g_vnorm[l], b_vnorm[l], w_spatial[l], b_spatial[l], w_out[l], g_ffn[l],
             w_query[l].astype(BF16), sub_keys1[l], sub_keys2[l],
             _pack_experts(expert_u, expert_v, l, tr=256).reshape(expert_u.shape[1], ROW_SUB, 128))
        last = l == depth - 1
        sc0 = jnp.zeros((nb, SCONV_W - 1, G_WIDTH), F32)
        cc0 = jnp.zeros((nb, CCONV_W - 1, G_WIDTH), F32)
        xp, (k1, v1, sc1, cc1, _) = _layer(xp, None, None, sc0, cc0, w, g_final, last)
        xs, (k2, v2, sc2, cc2, gv2) = _layer(xs, cache_k[l], cache_v[l], state_sconv[l], state_cconv[l],
                                             w, g_final, last)
        for lst, val in zip(outs, (k1, v1, k2, v2, sc1, sc2, cc1, cc2, gv2)):
            lst.append(val)
    kp, vp, ksm, vsm, scp, scs, ccp, ccs, gvs = [jnp.stack(o) for o in outs]
    return (xp, xs, kp, vp, ksm, vsm, scp, scs, ccp, ccs, gvs)
```

```python
import functools

import jax
import jax.numpy as jnp
from jax import lax
from jax.experimental import pallas as pl
from jax.experimental.pallas import tpu as pltpu

F32 = jnp.float32
BF16 = jnp.bfloat16
EPS = 1e-6

G_WIDTH = 512
N_HEADS = 8
HEAD_DIM = 64
SCONV_W = 3
CCONV_W = 31
MLP_CHUNK = 128
PEER_HEADS = 8
PEER_KEYS = 128
PEER_TOPK = 16
PEER_HALF = 128
N_SEL = PEER_HEADS * PEER_TOPK

VMEM_LIMIT_BYTES = 56 * 1024 * 1024


def _params(*sem):
    return pltpu.CompilerParams(dimension_semantics=sem, vmem_limit_bytes=VMEM_LIMIT_BYTES)


def _gelu_tanh(x):
    return 0.5 * x * (1.0 + jnp.tanh(0.7978845608028654 * (x + 0.044715 * (x * x * x))))


def _layernorm(x, g, b):
    xc = x - jnp.mean(x, axis=-1, keepdims=True)
    var = jnp.mean(xc * xc, axis=-1, keepdims=True)
    return xc * lax.rsqrt(var + EPS) * g + b


def _rmsnorm_rows(x, g):
    return x * lax.rsqrt(jnp.mean(x * x, axis=-1, keepdims=True) + EPS) * g


def _norm_matmul_kernel(x_ref, g_ref, w_ref, o_ref, h_ref):
    @pl.when(pl.program_id(1) == 0)
    def _():
        x = x_ref[...].reshape(h_ref.shape)
        h_ref[...] = _rmsnorm_rows(x, g_ref[...]).astype(BF16)

    o_ref[...] = jnp.dot(h_ref[...], w_ref[...], preferred_element_type=F32)


def _norm_matmul(x, g, w, *, tm, tn):
    n = x.shape[0]
    d, nout = w.shape
    xblock = (tm,) + x.shape[1:]
    return pl.pallas_call(
        _norm_matmul_kernel,
        grid=(n // tm, nout // tn),
        in_specs=[
            pl.BlockSpec(xblock, lambda i, j: (i,) + (0,) * (len(xblock) - 1)),
            pl.BlockSpec((1, d), lambda i, j: (0, 0)),
            pl.BlockSpec((d, tn), lambda i, j: (0, j)),
        ],
        out_specs=pl.BlockSpec((tm, tn), lambda i, j: (i, j)),
        out_shape=jax.ShapeDtypeStruct((n, nout), F32),
        scratch_shapes=[pltpu.VMEM((tm, d), BF16)],
        compiler_params=_params("parallel", "arbitrary"),
        name="norm_matmul",
    )(x, g.reshape(1, d), w)


RUN_CUTOFF = -104.0


def _sb_kernel(q_ref, k_ref, v_ref, o_ref, run_ref, acc_ref, q16_ref, *, tq, tk, q_offset):
    qi = pl.program_id(1)
    q_pos0 = q_offset + qi * tq
    n_kb = jnp.minimum(pl.cdiv(q_pos0 + tq, tk), k_ref.shape[1] // tk)

    run_ref[...] = jnp.zeros_like(run_ref)
    acc_ref[...] = jnp.zeros_like(acc_ref)
    for h in range(N_HEADS):
        q16_ref[h] = (q_ref[0, :, h * HEAD_DIM:(h + 1) * HEAD_DIM] * (HEAD_DIM ** -0.5)).astype(BF16)

    rows = N_HEADS * tq
    row = lax.broadcasted_iota(jnp.int32, (rows, tk), 0) & (tq - 1)
    col_minus_row = lax.broadcasted_iota(jnp.int32, (rows, tk), 1) - row
    col = lax.broadcasted_iota(jnp.int32, (2 * tk, 2 * tk), 1)
    krow = lax.broadcasted_iota(jnp.int32, (2 * tk, 2 * tk), 0) & (tk - 1)
    suffix_and_total = jnp.where((krow >= col) | (col >= tk), 1.0, 0.0).astype(BF16)

    def body(carry):
        it, _ = carry
        kb = n_kb - 1 - it
        k0 = pl.multiple_of(kb * tk, tk)
        mask = col_minus_row < (q_pos0 - k0)
        zs = []
        for h in range(N_HEADS):
            kh = k_ref[0, pl.ds(k0, tk), h * HEAD_DIM:(h + 1) * HEAD_DIM].astype(BF16)
            zs.append(lax.dot_general(q16_ref[h], kh, (((1,), (1,)), ((), ())), preferred_element_type=F32))
        z = jnp.concatenate(zs, axis=0)
        lf = -(jnp.maximum(z, 0.0) + jnp.log1p(jnp.exp(-jnp.abs(z))))
        lf = jnp.where(mask, lf, 0.0)
        lf_hi = lf.astype(BF16)
        lf_lo = (lf - lf_hi.astype(F32)).astype(BF16)
        sums = jnp.dot(jnp.concatenate([lf_hi, lf_lo], axis=1), suffix_and_total, preferred_element_type=F32)
        run = run_ref[...]
        w = jnp.where(mask, jnp.exp(z + sums[:, :tk] + run), 0.0).astype(BF16)
        for h in range(N_HEADS):
            vh = v_ref[0, pl.ds(k0, tk), h * HEAD_DIM:(h + 1) * HEAD_DIM].astype(BF16)
            acc_ref[h] += jnp.dot(w[h * tq:(h + 1) * tq], vh, preferred_element_type=F32)
        run = run + sums[:, tk:]
        run_ref[...] = run
        return it + 1, jnp.max(run)

    lax.while_loop(lambda c: (c[0] < n_kb) & (c[1] > RUN_CUTOFF), body, (0, 0.0))
    o_ref[0] = jnp.concatenate([acc_ref[h] for h in range(N_HEADS)], axis=-1)


def _sb_attention(q_arr, q_col, k_arr, k_col, v_arr, v_col, *, t_q, tq, tk, q_offset):
    b = q_arr.shape[0]
    t_k = k_arr.shape[1]
    kern = functools.partial(_sb_kernel, tq=tq, tk=tk, q_offset=q_offset)
    return pl.pallas_call(
        kern,
        grid=(b, t_q // tq),
        in_specs=[
            pl.BlockSpec((1, tq, G_WIDTH), lambda i, j: (i, j, q_col)),
            pl.BlockSpec((1, t_k, G_WIDTH), lambda i, j: (i, 0, k_col)),
            pl.BlockSpec((1, t_k, G_WIDTH), lambda i, j: (i, 0, v_col)),
        ],
        out_specs=pl.BlockSpec((1, tq, G_WIDTH), lambda i, j: (i, j, 0)),
        out_shape=jax.ShapeDtypeStruct((b, t_q, G_WIDTH), F32),
        scratch_shapes=[
            pltpu.VMEM((N_HEADS * tq, tk), F32),
            pltpu.VMEM((N_HEADS, tq, HEAD_DIM), F32),
            pltpu.VMEM((N_HEADS, tq, HEAD_DIM), BF16),
        ],
        compiler_params=_params("parallel", "arbitrary"),
        name="sb_attention",
    )(q_arr, k_arr, v_arr)


SC_HIST = 8
CC_HIST = 32


def _mixers_kernel(scb_ref, scc_ref, sch_ref, cfa_ref, cfg_ref, gmu_ref, gmv_ref,
                   sc_past_ref, cc_past_ref, wsc_ref, wcc_ref, bcc_ref, gcn_ref, bcn_ref,
                   gvn_ref, bvn_ref, wsp_ref, bsp_ref,
                   y_ref, vn_ref, sc_state_ref, cc_state_ref, u_buf, g_buf, *, tt, chunk):
    ti = pl.program_id(1)

    @pl.when(ti == 0)
    def _():
        u_buf[0:SC_HIST, :] = jnp.zeros((SC_HIST, G_WIDTH), F32)
        g_buf[0:CC_HIST, :] = jnp.zeros((CC_HIST, G_WIDTH), F32)
        u_buf[SC_HIST - (SCONV_W - 1):SC_HIST, :] = sc_past_ref[0]
        g_buf[CC_HIST - (CCONV_W - 1):CC_HIST, :] = cc_past_ref[0]

    u_buf[SC_HIST:SC_HIST + tt, :] = scc_ref[0] * sch_ref[0]
    conv_b = jnp.zeros((tt, G_WIDTH), F32)
    for k in range(SCONV_W):
        s0 = SC_HIST - (SCONV_W - 1) + k
        conv_b = conv_b + wsc_ref[k:k + 1, :] * u_buf[s0:s0 + tt, :]
    y_ref[0, :, 0:G_WIDTH] = scb_ref[0] * conv_b

    g_buf[CC_HIST:CC_HIST + tt, :] = cfa_ref[0] * jax.nn.sigmoid(cfg_ref[0])
    conv_c = jnp.zeros((tt, G_WIDTH), F32)
    for k in range(CCONV_W):
        s0 = CC_HIST - (CCONV_W - 1) + k
        conv_c = conv_c + wcc_ref[k:k + 1, :] * g_buf[s0:s0 + tt, :]
    ln = _layernorm(conv_c + bcc_ref[...], gcn_ref[...], bcn_ref[...])
    y_ref[0, :, G_WIDTH:2 * G_WIDTH] = ln * jax.nn.sigmoid(ln)

    vn = _layernorm(_gelu_tanh(gmv_ref[0]), gvn_ref[...], bvn_ref[...])
    vn_ref[0] = vn
    vn16 = vn.astype(BF16)
    for c in range(tt // chunk):
        rows = slice(c * chunk, (c + 1) * chunk)
        parts = []
        for h in range(N_HEADS):
            hs = slice(h * HEAD_DIM, (h + 1) * HEAD_DIM)
            parts.append(jnp.dot(wsp_ref[h], vn16[rows, hs], preferred_element_type=F32))
        s = jnp.concatenate(parts, axis=-1) + bsp_ref[...]
        y_ref[0, rows, 2 * G_WIDTH:3 * G_WIDTH] = _gelu_tanh(gmu_ref[0, rows, :]) * s

    sc_state_ref[0] = u_buf[SC_HIST + tt - (SCONV_W - 1):SC_HIST + tt, :]
    cc_state_ref[0] = g_buf[CC_HIST + tt - (CCONV_W - 1):CC_HIST + tt, :]
    u_buf[0:SC_HIST, :] = u_buf[tt:tt + SC_HIST, :]
    g_buf[0:CC_HIST, :] = g_buf[tt:tt + CC_HIST, :]


def _mixers(proj3, sc_past, cc_past, w_sconv, w_cconv, b_cconv, g_cnorm, b_cnorm, g_vnorm, b_vnorm,
            w_spatial, b_spatial, *, tt):
    b, t, _ = proj3.shape
    chunk = min(t, MLP_CHUNK)
    wsp = jnp.tril(w_spatial[:, :chunk, :chunk]).astype(BF16)
    bsp = jnp.repeat(b_spatial[:, :chunk].T, HEAD_DIM, axis=1)

    def col(c):
        return pl.BlockSpec((1, tt, G_WIDTH), lambda i, j: (i, j, c))

    def whole(a):
        return pl.BlockSpec(a.shape, lambda i, j: (0,) * a.ndim)

    def per_batch(a):
        return pl.BlockSpec((1,) + a.shape[1:], lambda i, j: (i,) + (0,) * (a.ndim - 1))

    row = lambda a: a.reshape(1, G_WIDTH)
    small = [w_sconv, w_cconv, row(b_cconv), row(g_cnorm), row(b_cnorm), row(g_vnorm), row(b_vnorm), wsp, bsp]
    kern = functools.partial(_mixers_kernel, tt=tt, chunk=chunk)
    return pl.pallas_call(
        kern,
        grid=(b, t // tt),
        in_specs=[col(c) for c in range(3, 10)] + [per_batch(sc_past), per_batch(cc_past)]
                 + [whole(a) for a in small],
        out_specs=[
            pl.BlockSpec((1, tt, 3 * G_WIDTH), lambda i, j: (i, j, 0)),
            pl.BlockSpec((1, tt, G_WIDTH), lambda i, j: (i, j, 0)),
            pl.BlockSpec((1, SCONV_W - 1, G_WIDTH), lambda i, j: (i, 0, 0)),
            pl.BlockSpec((1, CCONV_W - 1, G_WIDTH), lambda i, j: (i, 0, 0)),
        ],
        out_shape=[
            jax.ShapeDtypeStruct((b, t, 3 * G_WIDTH), F32),
            jax.ShapeDtypeStruct((b, t, G_WIDTH), F32),
            jax.ShapeDtypeStruct((b, SCONV_W - 1, G_WIDTH), F32),
            jax.ShapeDtypeStruct((b, CCONV_W - 1, G_WIDTH), F32),
        ],
        scratch_shapes=[pltpu.VMEM((SC_HIST + tt, G_WIDTH), F32), pltpu.VMEM((CC_HIST + tt, G_WIDTH), F32)],
        compiler_params=_params("parallel", "arbitrary"),
        name="mixers",
    )(*([proj3] * 7), sc_past, cc_past, *small)


def _out_proj_kernel(x_ref, ya_ref, yb_ref, wa_ref, wb_ref, o_ref):
    y = (x_ref[...]
         + jnp.dot(ya_ref[...].astype(BF16), wa_ref[...], preferred_element_type=F32)
         + jnp.dot(yb_ref[...].astype(BF16), wb_ref[...], preferred_element_type=F32))
    o_ref[...] = y.reshape(o_ref.shape)


def _out_proj(x, ya, yb, w_out, *, tm):
    n, d = x.shape
    slab = (d // 128, 128)
    wa = w_out[:G_WIDTH].astype(BF16)
    wb = w_out[G_WIDTH:].astype(BF16)
    return pl.pallas_call(
        _out_proj_kernel,
        grid=(n // tm,),
        in_specs=[
            pl.BlockSpec((tm, d), lambda i: (i, 0)),
            pl.BlockSpec((tm, G_WIDTH), lambda i: (i, 0)),
            pl.BlockSpec((tm, 3 * G_WIDTH), lambda i: (i, 0)),
            pl.BlockSpec(wa.shape, lambda i: (0, 0)),
            pl.BlockSpec(wb.shape, lambda i: (0, 0)),
        ],
        out_specs=pl.BlockSpec((tm,) + slab, lambda i: (i, 0, 0)),
        out_shape=jax.ShapeDtypeStruct((n,) + slab, F32),
        compiler_params=_params("parallel"),
        name="out_proj",
    )(x, ya, yb, wa, wb)


def _top16_rows(s, rid, rid_end):
    vals, idxs = [], []
    for _ in range(PEER_TOPK):
        m = jnp.max(s, axis=0, keepdims=True)
        i = jnp.min(jnp.where(s == m, rid, rid_end), axis=0, keepdims=True)
        s = jnp.where(rid == i, -jnp.inf, s)
        vals.append(m)
        idxs.append(i)
    return jnp.concatenate(vals, axis=0), jnp.concatenate(idxs, axis=0)


CAND_B = 8
CAND_ROWS = PEER_TOPK + (CAND_B - 1) * CAND_B + (PEER_TOPK - CAND_B)


def _pair_rows(x1, x2, combine):
    rows = [combine(x1[0:1], x2)]
    rows += [combine(x1[a:a + 1], x2[0:CAND_B]) for a in range(1, CAND_B)]
    rows.append(combine(x1[CAND_B:], x2[0:1]))
    return jnp.concatenate(rows, axis=0)


def _peer_topk_kernel(q_ref, k1_ref, k2_ref, eidx_ref, gate_ref):
    nt = (((1,), (1,)), ((), ()))
    tn = q_ref.shape[0]
    key_id = lax.broadcasted_iota(jnp.int32, (PEER_KEYS, tn), 0).astype(F32)
    r = lax.broadcasted_iota(jnp.int32, (CAND_ROWS, tn), 0)
    mid = r - PEER_TOPK
    last = r - (CAND_ROWS - (PEER_TOPK - CAND_B))
    flat_id = jnp.where(r < PEER_TOPK, r,
                        jnp.where(last < 0, ((mid >> 3) + 1) * PEER_TOPK + (mid & (CAND_B - 1)),
                                  (last + CAND_B) * PEER_TOPK)).astype(F32)
    assert CAND_B == 8
    e_rows, g_rows = [], []
    for h in range(PEER_HEADS):
        qa = q_ref[:, (2 * h) * PEER_HALF:(2 * h + 1) * PEER_HALF].astype(BF16)
        qb = q_ref[:, (2 * h + 1) * PEER_HALF:(2 * h + 2) * PEER_HALF].astype(BF16)
        s1 = lax.dot_general(k1_ref[h].astype(BF16), qa, nt, preferred_element_type=F32)
        s2 = lax.dot_general(k2_ref[h].astype(BF16), qb, nt, preferred_element_type=F32)
        t1, i1 = _top16_rows(s1, key_id, float(PEER_KEYS))
        t2, i2 = _top16_rows(s2, key_id, float(PEER_KEYS))
        cand = _pair_rows(t1, t2, lambda a, b: a + b)
        cidx = _pair_rows(i1, i2, lambda a, b: a * float(PEER_KEYS) + b)
        top, sel = _top16_rows(cand, flat_id, float(PEER_TOPK * PEER_TOPK))
        eidx = [jnp.sum(jnp.where(flat_id == sel[r:r + 1], cidx, 0.0), axis=0, keepdims=True)
                for r in range(PEER_TOPK)]
        p = jnp.exp(top - top[0:1])
        g_rows.append(p / jnp.sum(p, axis=0, keepdims=True))
        e_rows.append(jnp.concatenate(eidx, axis=0))
    eidx_ref[...] = jnp.concatenate(e_rows, axis=0).T.astype(jnp.int32)
    gate_ref[...] = jnp.concatenate(g_rows, axis=0).T


def _peer_topk(q, keys1, keys2, *, tn):
    n, d = q.shape
    return pl.pallas_call(
        _peer_topk_kernel,
        grid=(n // tn,),
        in_specs=[
            pl.BlockSpec((tn, d), lambda i: (i, 0)),
            pl.BlockSpec(keys1.shape, lambda i: (0, 0, 0)),
            pl.BlockSpec(keys2.shape, lambda i: (0, 0, 0)),
        ],
        out_specs=[pl.BlockSpec((tn, N_SEL), lambda i: (i, 0)), pl.BlockSpec((tn, N_SEL), lambda i: (i, 0))],
        out_shape=[jax.ShapeDtypeStruct((n, N_SEL), jnp.int32), jax.ShapeDtypeStruct((n, N_SEL), F32)],
        compiler_params=_params("parallel"),
        name="peer_topk",
    )(q, keys1, keys2)


N_SLOTS = 8
HI16 = -65536
ROW_SUB = 16


def _pack_experts_kernel(u_ref, v_ref, o_ref):
    ub = lax.bitcast_convert_type(u_ref[...].astype(BF16).astype(F32), jnp.int32)
    vb = lax.bitcast_convert_type(v_ref[...].astype(BF16).astype(F32), jnp.int32)
    o_ref[...] = (lax.shift_right_logical(ub, 16) | (vb & HI16)).reshape(o_ref.shape)


def _pack_experts(exp_u, exp_v, layer, *, tr):
    _, n, d = exp_u.shape
    spec = pl.BlockSpec((None, tr, d), lambda i: (layer, i, 0))
    return pl.pallas_call(
        _pack_experts_kernel,
        grid=(n // tr,),
        in_specs=[spec, spec],
        out_specs=pl.BlockSpec((tr, ROW_SUB, 128), lambda i: (i, 0, 0)),
        out_shape=jax.ShapeDtypeStruct((n, ROW_SUB, 128), jnp.int32),
        compiler_params=_params("parallel"),
        name="pack_experts",
    )(exp_u, exp_v)


def _sublane_sums8(ps):
    sid = lax.broadcasted_iota(jnp.int32, (8, 128), 0)
    m4 = sid < 4
    m2 = (sid & 2) == 0
    m1 = (sid & 1) == 0
    b = [jnp.where(m4, ps[k], ps[k + 4]) + pltpu.roll(jnp.where(m4, ps[k + 4], ps[k]), 4, 0) for k in range(4)]
    c = [jnp.where(m2, b[k], b[k + 2]) + jnp.where(m2, pltpu.roll(b[k], 6, 0), pltpu.roll(b[k + 2], 2, 0))
         for k in range(2)]
    return jnp.where(m1, c[0], c[1]) + jnp.where(m1, pltpu.roll(c[0], 7, 0), pltpu.roll(c[1], 1, 0))


def _block_rms_scale(x):
    ss = jnp.sum(jnp.sum(x * x, axis=2, keepdims=True), axis=1, keepdims=True)
    return lax.rsqrt(ss * (1.0 / (ROW_SUB * 128)) + EPS)


def _peer_experts_kernel(eidx_ref, x_ref, gate_ref, g_ref, gf_ref, uv_hbm, o_ref, buf, sem, ga_ref, h_ref, y_ref,
                         *, tb, final_norm):
    n_groups = N_SEL // 8
    per = N_SEL // (2 * n_groups)
    lookahead = N_SLOTS - 1

    def issue(t, slot, lo, hi):
        for m in range(lo, hi):
            e = eidx_ref[t, m]
            pltpu.make_async_copy(uv_hbm.at[pl.ds(e, 1)], buf.at[slot, pl.ds(m, 1)],
                                  sem.at[slot]).start(priority=m % 2)

    def wait(slot):
        pltpu.make_async_copy(uv_hbm.at[pl.ds(0, N_SEL)], buf.at[slot], sem.at[slot]).wait()

    eye = (lax.broadcasted_iota(jnp.int32, (N_SEL, N_SEL), 0)
           == lax.broadcasted_iota(jnp.int32, (N_SEL, N_SEL), 1))

    def stage_a_groups(t, slot, issue_next):
        h0, h1 = h_ref[t, 0:8, :], h_ref[t, 8:16, :]
        qs = []
        for gi in range(n_groups):
            issue_next(gi * per, (gi + 1) * per)
            ps = []
            for j in range(8):
                m = gi * 8 + j
                u0 = lax.bitcast_convert_type(lax.shift_left(buf[slot, m, 0:8, :], 16), F32)
                u1 = lax.bitcast_convert_type(lax.shift_left(buf[slot, m, 8:16, :], 16), F32)
                ps.append(u0 * h0 + u1 * h1)
            qs.append(_sublane_sums8(ps))
        return qs

    def stage_a_tail(t, par, qs):
        act = _gelu_tanh(jnp.sum(jnp.concatenate(qs, axis=0), axis=-1, keepdims=True))
        gate_col = jnp.sum(jnp.where(eye, gate_ref[pl.ds(t, 1), :], 0.0), axis=-1, keepdims=True)
        ga_ref[par] = jnp.broadcast_to(gate_col * act, (N_SEL, 128))

    def stage_b(t, slot, par, issue_next):
        acc0 = jnp.zeros((8, 128), F32)
        acc1 = jnp.zeros((8, 128), F32)
        for m in range(N_SEL):
            if m % 8 == 0:
                gi = m // 8
                issue_next(N_SEL // 2 + gi * per, N_SEL // 2 + (gi + 1) * per)
            gm = jnp.broadcast_to(ga_ref[par, pl.ds(m, 1), :], (8, 128))
            acc0 = acc0 + gm * lax.bitcast_convert_type(buf[slot, m, 0:8, :] & HI16, F32)
            acc1 = acc1 + gm * lax.bitcast_convert_type(buf[slot, m, 8:16, :] & HI16, F32)
        y_ref[t] = x_ref[t] + jnp.concatenate([acc0, acc1], axis=0)

    def no_issue(lo, hi):
        del lo, hi

    def token(t, j, with_issue):
        issue_next = (lambda lo, hi: issue(t + lookahead, (j + lookahead) % N_SLOTS, lo, hi)) if with_issue else no_issue
        wait((j + 1) % N_SLOTS)
        qs = stage_a_groups(t + 1, (j + 1) % N_SLOTS, issue_next)
        stage_b(t, j, j % 2, issue_next)
        stage_a_tail(t + 1, (j + 1) % 2, qs)

    for t in range(lookahead):
        issue(t, t, 0, N_SEL)
    xb = x_ref[...]
    h_ref[...] = xb * _block_rms_scale(xb) * g_ref[...]
    wait(0)
    stage_a_tail(0, 0, stage_a_groups(0, 0, no_issue))

    n_full = (tb - lookahead) // N_SLOTS

    def steady(i, carry):
        for j in range(N_SLOTS):
            token(i * N_SLOTS + j, j, True)
        return carry

    lax.fori_loop(0, n_full, steady, 0)
    for t in range(n_full * N_SLOTS, tb - 1):
        token(t, t % N_SLOTS, t + lookahead < tb)
    stage_b(tb - 1, (tb - 1) % N_SLOTS, (tb - 1) % 2, no_issue)
    yb = y_ref[...]
    if final_norm:
        yb = yb * _block_rms_scale(yb) * gf_ref[...]
    o_ref[...] = yb.reshape(o_ref.shape)


def _peer_experts(x, eidx, gate, g_ffn, g_final, exp_uv, *, tb, final_norm):
    n = x.shape[0]
    slab = (ROW_SUB, 128)
    d = ROW_SUB * 128
    xspec = pl.BlockSpec((tb,) + slab, lambda i: (i, 0, 0))
    gspec = pl.BlockSpec(slab, lambda i: (0, 0))
    kern = functools.partial(_peer_experts_kernel, tb=tb, final_norm=final_norm)
    return pl.pallas_call(
        kern,
        grid=(n // tb,),
        in_specs=[
            pl.BlockSpec((tb, N_SEL), lambda i: (i, 0), memory_space=pltpu.SMEM),
            xspec,
            pl.BlockSpec((tb, N_SEL), lambda i: (i, 0)),
            gspec,
            gspec,
            pl.BlockSpec(memory_space=pl.ANY),
        ],
        out_specs=pl.BlockSpec((tb, d), lambda i: (i, 0)),
        out_shape=jax.ShapeDtypeStruct((n, d), F32),
        scratch_shapes=[
            pltpu.VMEM((N_SLOTS, N_SEL) + slab, jnp.int32),
            pltpu.SemaphoreType.DMA((N_SLOTS,)),
            pltpu.VMEM((2, N_SEL, 128), F32),
            pltpu.VMEM((tb,) + slab, F32),
            pltpu.VMEM((tb,) + slab, F32),
        ],
        compiler_params=_params("arbitrary"),
        name="peer_experts",
    )(eidx, x, gate, g_ffn.reshape(slab), g_final.reshape(slab), exp_uv)


def _layer(x3, k_past, v_past, sc_past, cc_past, w, g_final, final_norm):
    (g_mix, w_in16, w_sconv, w_cconv, b_cconv, g_cnorm, b_cnorm, g_vnorm, b_vnorm,
     w_spatial, b_spatial, w_out, g_ffn, w_query16, keys1, keys2, exp_uv) = w
    b, t, d = x3.shape
    n = b * t
    x = x3.reshape(n, d)
    tm = min(512, n)

    proj = _norm_matmul(x, g_mix, w_in16, tm=tm, tn=1280)
    proj3 = proj.reshape(b, t, proj.shape[1])
    k_new = proj3[:, :, G_WIDTH:2 * G_WIDTH]
    v_new = proj3[:, :, 2 * G_WIDTH:3 * G_WIDTH]
    if k_past is None:
        y_a = _sb_attention(proj3, 0, proj3, 1, proj3, 2, t_q=t, tq=128, tk=128, q_offset=0)
    else:
        past = k_past.shape[1]
        k_all = jnp.concatenate([k_past.reshape(b, past, G_WIDTH), k_new], axis=1)
        v_all = jnp.concatenate([v_past.reshape(b, past, G_WIDTH), v_new], axis=1)
        y_a = _sb_attention(proj3, 0, k_all, 0, v_all, 0, t_q=t, tq=t, tk=t, q_offset=past)
    y_bcd, vn, sc_state, cc_state = _mixers(
        proj3, sc_past, cc_past, w_sconv, w_cconv, b_cconv, g_cnorm, b_cnorm, g_vnorm, b_vnorm,
        w_spatial, b_spatial, tt=min(512, t))
    x1 = _out_proj(x, y_a.reshape(n, G_WIDTH), y_bcd.reshape(n, 3 * G_WIDTH), w_out, tm=min(256, n))
    q = _norm_matmul(x1, g_ffn, w_query16, tm=tm, tn=1024)
    eidx, gate = _peer_topk(q, keys1, keys2, tn=min(512, n))
    x2 = _peer_experts(x1, eidx, gate, g_ffn, g_final, exp_uv, tb=min(256, n), final_norm=final_norm)
    heads = (b, t, N_HEADS, HEAD_DIM)
    return x2.reshape(b, t, d), (k_new.reshape(heads), v_new.reshape(heads), sc_state, cc_state, vn)


def kernel(x_prompt, x_sample, cache_k, cache_v, state_sconv, state_cconv, g_mix, w_in, w_sconv, w_cconv,
           b_cconv, g_cnorm, b_cnorm, g_vnorm, b_vnorm, w_spatial, b_spatial, w_out, g_ffn, w_query,
           sub_keys1, sub_keys2, expert_u, expert_v, g_final):
    depth = w_in.shape[0]
    xp, xs = x_prompt, x_sample
    nb = xp.shape[0]
    outs = [[] for _ in range(9)]
    for l in range(depth):
        w = (g_mix[l], w_in[l].astype(BF16), w_sconv[l], w_cconv[l], b_cconv[l], g_cnorm[l], b_cnorm[l],
             g_vnorm[l], b_vnorm[l], w_spatial[l], b_spatial[l], w_out[l], g_ffn[l],
             w_query[l].astype(BF16), sub_keys1[l], sub_keys2[l],
             _pack_experts(expert_u, expert_v, l, tr=256))
        last = l == depth - 1
        sc0 = jnp.zeros((nb, SCONV_W - 1, G_WIDTH), F32)
        cc0 = jnp.zeros((nb, CCONV_W - 1, G_WIDTH), F32)
        xp, (k1, v1, sc1, cc1, _) = _layer(xp, None, None, sc0, cc0, w, g_final, last)
        xs, (k2, v2, sc2, cc2, gv2) = _layer(xs, cache_k[l], cache_v[l], state_sconv[l], state_cconv[l],
                                             w, g_final, last)
        for lst, val in zip(outs, (k1, v1, k2, v2, sc1, sc2, cc1, cc2, gv2)):
            lst.append(val)
    kp, vp, ksm, vsm, scp, scs, ccp, ccs, gvs = [jnp.stack(o) for o in outs]
    return (xp, xs, kp, vp, ksm, vsm, scp, scs, ccp, ccs, gvs)
```

```python
import functools

import jax
import jax.numpy as jnp
from jax import lax
from jax.experimental import pallas as pl
from jax.experimental.pallas import tpu as pltpu

F32 = jnp.float32
BF16 = jnp.bfloat16
EPS = 1e-6

G_WIDTH = 512
N_HEADS = 8
HEAD_DIM = 64
SCONV_W = 3
CCONV_W = 31
MLP_CHUNK = 128
PEER_HEADS = 8
PEER_KEYS = 128
PEER_TOPK = 16
PEER_HALF = 128
N_SEL = PEER_HEADS * PEER_TOPK

VMEM_LIMIT_BYTES = 56 * 1024 * 1024


def _params(*sem):
    return pltpu.CompilerParams(dimension_semantics=sem, vmem_limit_bytes=VMEM_LIMIT_BYTES)


def _gelu_tanh(x):
    return 0.5 * x * (1.0 + jnp.tanh(0.7978845608028654 * (x + 0.044715 * (x * x * x))))


def _layernorm(x, g, b):
    xc = x - jnp.mean(x, axis=-1, keepdims=True)
    var = jnp.mean(xc * xc, axis=-1, keepdims=True)
    return xc * lax.rsqrt(var + EPS) * g + b


def _rmsnorm_rows(x, g):
    return x * lax.rsqrt(jnp.mean(x * x, axis=-1, keepdims=True) + EPS) * g


def _norm_matmul_kernel(x_ref, g_ref, w_ref, o_ref, h_ref):
    @pl.when(pl.program_id(1) == 0)
    def _():
        x = x_ref[...].reshape(h_ref.shape)
        h_ref[...] = _rmsnorm_rows(x, g_ref[...]).astype(BF16)

    o_ref[...] = jnp.dot(h_ref[...], w_ref[...], preferred_element_type=F32)


def _norm_matmul(x, g, w, *, tm, tn):
    n = x.shape[0]
    d, nout = w.shape
    xblock = (tm,) + x.shape[1:]
    return pl.pallas_call(
        _norm_matmul_kernel,
        grid=(n // tm, nout // tn),
        in_specs=[
            pl.BlockSpec(xblock, lambda i, j: (i,) + (0,) * (len(xblock) - 1)),
            pl.BlockSpec((1, d), lambda i, j: (0, 0)),
            pl.BlockSpec((d, tn), lambda i, j: (0, j)),
        ],
        out_specs=pl.BlockSpec((tm, tn), lambda i, j: (i, j)),
        out_shape=jax.ShapeDtypeStruct((n, nout), F32),
        scratch_shapes=[pltpu.VMEM((tm, d), BF16)],
        compiler_params=_params("parallel", "arbitrary"),
        name="norm_matmul",
    )(x, g.reshape(1, d), w)


RUN_CUTOFF = -104.0


def _sb_kernel(q_ref, k_ref, v_ref, o_ref, run_ref, acc_ref, q16_ref, *, tq, tk, q_offset):
    qi = pl.program_id(1)
    q_pos0 = q_offset + qi * tq
    n_kb = jnp.minimum(pl.cdiv(q_pos0 + tq, tk), k_ref.shape[1] // tk)

    run_ref[...] = jnp.zeros_like(run_ref)
    acc_ref[...] = jnp.zeros_like(acc_ref)
    for h in range(N_HEADS):
        q16_ref[h] = (q_ref[0, :, h * HEAD_DIM:(h + 1) * HEAD_DIM] * (HEAD_DIM ** -0.5)).astype(BF16)

    rows = N_HEADS * tq
    row = lax.broadcasted_iota(jnp.int32, (rows, tk), 0) & (tq - 1)
    col_minus_row = lax.broadcasted_iota(jnp.int32, (rows, tk), 1) - row
    col = lax.broadcasted_iota(jnp.int32, (2 * tk, 2 * tk), 1)
    krow = lax.broadcasted_iota(jnp.int32, (2 * tk, 2 * tk), 0) & (tk - 1)
    suffix_and_total = jnp.where((krow >= col) | (col >= tk), 1.0, 0.0).astype(BF16)

    def body(carry):
        it, _ = carry
        kb = n_kb - 1 - it
        k0 = pl.multiple_of(kb * tk, tk)
        mask = col_minus_row < (q_pos0 - k0)
        zs = []
        for h in range(N_HEADS):
            kh = k_ref[0, pl.ds(k0, tk), h * HEAD_DIM:(h + 1) * HEAD_DIM].astype(BF16)
            zs.append(lax.dot_general(q16_ref[h], kh, (((1,), (1,)), ((), ())), preferred_element_type=F32))
        z = jnp.concatenate(zs, axis=0)
        lf = -(jnp.maximum(z, 0.0) + jnp.log1p(jnp.exp(-jnp.abs(z))))
        lf = jnp.where(mask, lf, 0.0)
        lf_hi = lf.astype(BF16)
        lf_lo = (lf - lf_hi.astype(F32)).astype(BF16)
        sums = jnp.dot(jnp.concatenate([lf_hi, lf_lo], axis=1), suffix_and_total, preferred_element_type=F32)
        run = run_ref[...]
        w = jnp.where(mask, jnp.exp(z + sums[:, :tk] + run), 0.0).astype(BF16)
        for h in range(N_HEADS):
            vh = v_ref[0, pl.ds(k0, tk), h * HEAD_DIM:(h + 1) * HEAD_DIM].astype(BF16)
            acc_ref[h] += jnp.dot(w[h * tq:(h + 1) * tq], vh, preferred_element_type=F32)
        run = run + sums[:, tk:]
        run_ref[...] = run
        return it + 1, jnp.max(run)

    lax.while_loop(lambda c: (c[0] < n_kb) & (c[1] > RUN_CUTOFF), body, (0, 0.0))
    o_ref[0] = jnp.concatenate([acc_ref[h] for h in range(N_HEADS)], axis=-1)


def _sb_attention(q_arr, q_col, k_arr, k_col, v_arr, v_col, *, t_q, tq, tk, q_offset):
    b = q_arr.shape[0]
    t_k = k_arr.shape[1]
    kern = functools.partial(_sb_kernel, tq=tq, tk=tk, q_offset=q_offset)
    return pl.pallas_call(
        kern,
        grid=(b, t_q // tq),
        in_specs=[
            pl.BlockSpec((1, tq, G_WIDTH), lambda i, j: (i, j, q_col)),
            pl.BlockSpec((1, t_k, G_WIDTH), lambda i, j: (i, 0, k_col)),
            pl.BlockSpec((1, t_k, G_WIDTH), lambda i, j: (i, 0, v_col)),
        ],
        out_specs=pl.BlockSpec((1, tq, G_WIDTH), lambda i, j: (i, j, 0)),
        out_shape=jax.ShapeDtypeStruct((b, t_q, G_WIDTH), F32),
        scratch_shapes=[
            pltpu.VMEM((N_HEADS * tq, tk), F32),
            pltpu.VMEM((N_HEADS, tq, HEAD_DIM), F32),
            pltpu.VMEM((N_HEADS, tq, HEAD_DIM), BF16),
        ],
        compiler_params=_params("parallel", "arbitrary"),
        name="sb_attention",
    )(q_arr, k_arr, v_arr)


SC_HIST = 8
CC_HIST = 32


def _mixers_kernel(scb_ref, scc_ref, sch_ref, cfa_ref, cfg_ref, gmu_ref, gmv_ref,
                   sc_past_ref, cc_past_ref, wsc_ref, wcc_ref, bcc_ref, gcn_ref, bcn_ref,
                   gvn_ref, bvn_ref, wsp_ref, bsp_ref,
                   y_ref, vn_ref, sc_state_ref, cc_state_ref, u_buf, g_buf, *, tt, chunk):
    ti = pl.program_id(1)

    @pl.when(ti == 0)
    def _():
        u_buf[0:SC_HIST, :] = jnp.zeros((SC_HIST, G_WIDTH), F32)
        g_buf[0:CC_HIST, :] = jnp.zeros((CC_HIST, G_WIDTH), F32)
        u_buf[SC_HIST - (SCONV_W - 1):SC_HIST, :] = sc_past_ref[0]
        g_buf[CC_HIST - (CCONV_W - 1):CC_HIST, :] = cc_past_ref[0]

    u_buf[SC_HIST:SC_HIST + tt, :] = scc_ref[0] * sch_ref[0]
    conv_b = jnp.zeros((tt, G_WIDTH), F32)
    for k in range(SCONV_W):
        s0 = SC_HIST - (SCONV_W - 1) + k
        conv_b = conv_b + wsc_ref[k:k + 1, :] * u_buf[s0:s0 + tt, :]
    y_ref[0, :, 0:G_WIDTH] = scb_ref[0] * conv_b

    g_buf[CC_HIST:CC_HIST + tt, :] = cfa_ref[0] * jax.nn.sigmoid(cfg_ref[0])
    conv_c = jnp.zeros((tt, G_WIDTH), F32)
    for k in range(CCONV_W):
        s0 = CC_HIST - (CCONV_W - 1) + k
        conv_c = conv_c + wcc_ref[k:k + 1, :] * g_buf[s0:s0 + tt, :]
    ln = _layernorm(conv_c + bcc_ref[...], gcn_ref[...], bcn_ref[...])
    y_ref[0, :, G_WIDTH:2 * G_WIDTH] = ln * jax.nn.sigmoid(ln)

    vn = _layernorm(_gelu_tanh(gmv_ref[0]), gvn_ref[...], bvn_ref[...])
    vn_ref[0] = vn
    vn16 = vn.astype(BF16)
    for c in range(tt // chunk):
        rows = slice(c * chunk, (c + 1) * chunk)
        parts = []
        for h in range(N_HEADS):
            hs = slice(h * HEAD_DIM, (h + 1) * HEAD_DIM)
            parts.append(jnp.dot(wsp_ref[h], vn16[rows, hs], preferred_element_type=F32))
        s = jnp.concatenate(parts, axis=-1) + bsp_ref[...]
        y_ref[0, rows, 2 * G_WIDTH:3 * G_WIDTH] = _gelu_tanh(gmu_ref[0, rows, :]) * s

    sc_state_ref[0] = u_buf[SC_HIST + tt - (SCONV_W - 1):SC_HIST + tt, :]
    cc_state_ref[0] = g_buf[CC_HIST + tt - (CCONV_W - 1):CC_HIST + tt, :]
    u_buf[0:SC_HIST, :] = u_buf[tt:tt + SC_HIST, :]
    g_buf[0:CC_HIST, :] = g_buf[tt:tt + CC_HIST, :]


def _mixers(proj3, sc_past, cc_past, w_sconv, w_cconv, b_cconv, g_cnorm, b_cnorm, g_vnorm, b_vnorm,
            w_spatial, b_spatial, *, tt):
    b, t, _ = proj3.shape
    chunk = min(t, MLP_CHUNK)
    wsp = jnp.tril(w_spatial[:, :chunk, :chunk]).astype(BF16)
    bsp = jnp.repeat(b_spatial[:, :chunk].T, HEAD_DIM, axis=1)

    def col(c):
        return pl.BlockSpec((1, tt, G_WIDTH), lambda i, j: (i, j, c))

    def whole(a):
        return pl.BlockSpec(a.shape, lambda i, j: (0,) * a.ndim)

    def per_batch(a):
        return pl.BlockSpec((1,) + a.shape[1:], lambda i, j: (i,) + (0,) * (a.ndim - 1))

    row = lambda a: a.reshape(1, G_WIDTH)
    small = [w_sconv, w_cconv, row(b_cconv), row(g_cnorm), row(b_cnorm), row(g_vnorm), row(b_vnorm), wsp, bsp]
    kern = functools.partial(_mixers_kernel, tt=tt, chunk=chunk)
    return pl.pallas_call(
        kern,
        grid=(b, t // tt),
        in_specs=[col(c) for c in range(3, 10)] + [per_batch(sc_past), per_batch(cc_past)]
                 + [whole(a) for a in small],
        out_specs=[
            pl.BlockSpec((1, tt, 3 * G_WIDTH), lambda i, j: (i, j, 0)),
            pl.BlockSpec((1, tt, G_WIDTH), lambda i, j: (i, j, 0)),
            pl.BlockSpec((1, SCONV_W - 1, G_WIDTH), lambda i, j: (i, 0, 0)),
            pl.BlockSpec((1, CCONV_W - 1, G_WIDTH), lambda i, j: (i, 0, 0)),
        ],
        out_shape=[
            jax.ShapeDtypeStruct((b, t, 3 * G_WIDTH), F32),
            jax.ShapeDtypeStruct((b, t, G_WIDTH), F32),
            jax.ShapeDtypeStruct((b, SCONV_W - 1, G_WIDTH), F32),
            jax.ShapeDtypeStruct((b, CCONV_W - 1, G_WIDTH), F32),
        ],
        scratch_shapes=[pltpu.VMEM((SC_HIST + tt, G_WIDTH), F32), pltpu.VMEM((CC_HIST + tt, G_WIDTH), F32)],
        compiler_params=_params("parallel", "arbitrary"),
        name="mixers",
    )(*([proj3] * 7), sc_past, cc_past, *small)


def _out_proj_kernel(x_ref, ya_ref, yb_ref, wa_ref, wb_ref, o_ref):
    y = (x_ref[...]
         + jnp.dot(ya_ref[...].astype(BF16), wa_ref[...], preferred_element_type=F32)
         + jnp.dot(yb_ref[...].astype(BF16), wb_ref[...], preferred_element_type=F32))
    o_ref[...] = y.reshape(o_ref.shape)


def _out_proj(x, ya, yb, w_out, *, tm):
    n, d = x.shape
    slab = (d // 128, 128)
    wa = w_out[:G_WIDTH].astype(BF16)
    wb = w_out[G_WIDTH:].astype(BF16)
    return pl.pallas_call(
        _out_proj_kernel,
        grid=(n // tm,),
        in_specs=[
            pl.BlockSpec((tm, d), lambda i: (i, 0)),
            pl.BlockSpec((tm, G_WIDTH), lambda i: (i, 0)),
            pl.BlockSpec((tm, 3 * G_WIDTH), lambda i: (i, 0)),
            pl.BlockSpec(wa.shape, lambda i: (0, 0)),
            pl.BlockSpec(wb.shape, lambda i: (0, 0)),
        ],
        out_specs=pl.BlockSpec((tm,) + slab, lambda i: (i, 0, 0)),
        out_shape=jax.ShapeDtypeStruct((n,) + slab, F32),
        compiler_params=_params("parallel"),
        name="out_proj",
    )(x, ya, yb, wa, wb)


def _top16_rows(s, rid, rid_end):
    vals, idxs = [], []
    for _ in range(PEER_TOPK):
        m = jnp.max(s, axis=0, keepdims=True)
        i = jnp.min(jnp.where(s == m, rid, rid_end), axis=0, keepdims=True)
        s = jnp.where(rid == i, -jnp.inf, s)
        vals.append(m)
        idxs.append(i)
    return jnp.concatenate(vals, axis=0), jnp.concatenate(idxs, axis=0)


CAND_B = 8
CAND_ROWS = PEER_TOPK + (CAND_B - 1) * CAND_B + (PEER_TOPK - CAND_B)


def _pair_rows(x1, x2, combine):
    rows = [combine(x1[0:1], x2)]
    rows += [combine(x1[a:a + 1], x2[0:CAND_B]) for a in range(1, CAND_B)]
    rows.append(combine(x1[CAND_B:], x2[0:1]))
    return jnp.concatenate(rows, axis=0)


def _peer_topk_kernel(q_ref, k1_ref, k2_ref, eidx_ref, gate_ref):
    nt = (((1,), (1,)), ((), ()))
    tn = q_ref.shape[0]
    key_id = lax.broadcasted_iota(jnp.int32, (PEER_KEYS, tn), 0).astype(F32)
    r = lax.broadcasted_iota(jnp.int32, (CAND_ROWS, tn), 0)
    mid = r - PEER_TOPK
    last = r - (CAND_ROWS - (PEER_TOPK - CAND_B))
    flat_id = jnp.where(r < PEER_TOPK, r,
                        jnp.where(last < 0, ((mid >> 3) + 1) * PEER_TOPK + (mid & (CAND_B - 1)),
                                  (last + CAND_B) * PEER_TOPK)).astype(F32)
    assert CAND_B == 8
    e_rows, g_rows = [], []
    for h in range(PEER_HEADS):
        qa = q_ref[:, (2 * h) * PEER_HALF:(2 * h + 1) * PEER_HALF].astype(BF16)
        qb = q_ref[:, (2 * h + 1) * PEER_HALF:(2 * h + 2) * PEER_HALF].astype(BF16)
        s1 = lax.dot_general(k1_ref[h].astype(BF16), qa, nt, preferred_element_type=F32)
        s2 = lax.dot_general(k2_ref[h].astype(BF16), qb, nt, preferred_element_type=F32)
        t1, i1 = _top16_rows(s1, key_id, float(PEER_KEYS))
        t2, i2 = _top16_rows(s2, key_id, float(PEER_KEYS))
        cand = _pair_rows(t1, t2, lambda a, b: a + b)
        cidx = _pair_rows(i1, i2, lambda a, b: a * float(PEER_KEYS) + b)
        top, sel = _top16_rows(cand, flat_id, float(PEER_TOPK * PEER_TOPK))
        eidx = [jnp.sum(jnp.where(flat_id == sel[r:r + 1], cidx, 0.0), axis=0, keepdims=True)
                for r in range(PEER_TOPK)]
        p = jnp.exp(top - top[0:1])
        g_rows.append(p / jnp.sum(p, axis=0, keepdims=True))
        e_rows.append(jnp.concatenate(eidx, axis=0))
    eidx_ref[...] = jnp.concatenate(e_rows, axis=0).T.astype(jnp.int32)
    gate_ref[...] = jnp.concatenate(g_rows, axis=0).T


def _peer_topk(q, keys1, keys2, *, tn):
    n, d = q.shape
    return pl.pallas_call(
        _peer_topk_kernel,
        grid=(n // tn,),
        in_specs=[
            pl.BlockSpec((tn, d), lambda i: (i, 0)),
            pl.BlockSpec(keys1.shape, lambda i: (0, 0, 0)),
            pl.BlockSpec(keys2.shape, lambda i: (0, 0, 0)),
        ],
        out_specs=[pl.BlockSpec((tn, N_SEL), lambda i: (i, 0)), pl.BlockSpec((tn, N_SEL), lambda i: (i, 0))],
        out_shape=[jax.ShapeDtypeStruct((n, N_SEL), jnp.int32), jax.ShapeDtypeStruct((n, N_SEL), F32)],
        compiler_params=_params("parallel"),
        name="peer_topk",
    )(q, keys1, keys2)


N_SLOTS = 8
HI16 = -65536
ROW_SUB = 16


def _pack_experts_kernel(u_ref, v_ref, o_ref):
    ub = lax.bitcast_convert_type(u_ref[...].astype(BF16).astype(F32), jnp.int32)
    vb = lax.bitcast_convert_type(v_ref[...].astype(BF16).astype(F32), jnp.int32)
    o_ref[...] = (lax.shift_right_logical(ub, 16) | (vb & HI16)).reshape(o_ref.shape)


def _pack_experts(exp_u, exp_v, layer, *, tr):
    _, n, d = exp_u.shape
    spec = pl.BlockSpec((None, tr, d), lambda i: (layer, i, 0))
    return pl.pallas_call(
        _pack_experts_kernel,
        grid=(n // tr,),
        in_specs=[spec, spec],
        out_specs=pl.BlockSpec((tr, ROW_SUB, 128), lambda i: (i, 0, 0)),
        out_shape=jax.ShapeDtypeStruct((n, ROW_SUB, 128), jnp.int32),
        compiler_params=_params("parallel"),
        name="pack_experts",
    )(exp_u, exp_v)


def _sublane_sums8(ps):
    sid = lax.broadcasted_iota(jnp.int32, (8, 128), 0)
    m4 = sid < 4
    m2 = (sid & 2) == 0
    m1 = (sid & 1) == 0
    b = [jnp.where(m4, ps[k], ps[k + 4]) + pltpu.roll(jnp.where(m4, ps[k + 4], ps[k]), 4, 0) for k in range(4)]
    c = [jnp.where(m2, b[k], b[k + 2]) + jnp.where(m2, pltpu.roll(b[k], 6, 0), pltpu.roll(b[k + 2], 2, 0))
         for k in range(2)]
    return jnp.where(m1, c[0], c[1]) + jnp.where(m1, pltpu.roll(c[0], 7, 0), pltpu.roll(c[1], 1, 0))


def _block_rms_scale(x):
    ss = jnp.sum(jnp.sum(x * x, axis=2, keepdims=True), axis=1, keepdims=True)
    return lax.rsqrt(ss * (1.0 / (ROW_SUB * 128)) + EPS)


def _peer_experts_kernel(eidx_ref, x_ref, gate_ref, g_ref, gf_ref, uv_hbm, o_ref, buf, sem, ga_ref, h_ref, y_ref,
                         *, tb, final_norm):
    n_groups = N_SEL // 8
    per = N_SEL // (2 * n_groups)
    lookahead = N_SLOTS - 1

    def issue(t, slot, lo, hi):
        for m in range(lo, hi):
            e = eidx_ref[t, m]
            pltpu.make_async_copy(uv_hbm.at[pl.ds(e, 1)], buf.at[slot, pl.ds(m, 1)],
                                  sem.at[slot]).start(priority=m % 2)

    def wait(slot):
        pltpu.make_async_copy(uv_hbm.at[pl.ds(0, N_SEL)], buf.at[slot], sem.at[slot]).wait()

    eye = (lax.broadcasted_iota(jnp.int32, (N_SEL, N_SEL), 0)
           == lax.broadcasted_iota(jnp.int32, (N_SEL, N_SEL), 1))

    def stage_a_groups(t, slot, issue_next):
        h0, h1 = h_ref[t, 0:8, :], h_ref[t, 8:16, :]
        qs = []
        for gi in range(n_groups):
            issue_next(gi * per, (gi + 1) * per)
            ps = []
            for j in range(8):
                m = gi * 8 + j
                u0 = lax.bitcast_convert_type(lax.shift_left(buf[slot, m, 0:8, :], 16), F32)
                u1 = lax.bitcast_convert_type(lax.shift_left(buf[slot, m, 8:16, :], 16), F32)
                ps.append(u0 * h0 + u1 * h1)
            qs.append(_sublane_sums8(ps))
        return qs

    def stage_a_tail(t, par, qs):
        act = _gelu_tanh(jnp.sum(jnp.concatenate(qs, axis=0), axis=-1, keepdims=True))
        gate_col = jnp.sum(jnp.where(eye, gate_ref[pl.ds(t, 1), :], 0.0), axis=-1, keepdims=True)
        ga_ref[par] = jnp.broadcast_to(gate_col * act, (N_SEL, 128))

    def stage_b(t, slot, par, issue_next):
        acc0 = jnp.zeros((8, 128), F32)
        acc1 = jnp.zeros((8, 128), F32)
        for m in range(N_SEL):
            if m % 8 == 0:
                gi = m // 8
                issue_next(N_SEL // 2 + gi * per, N_SEL // 2 + (gi + 1) * per)
            gm = jnp.broadcast_to(ga_ref[par, pl.ds(m, 1), :], (8, 128))
            acc0 = acc0 + gm * lax.bitcast_convert_type(buf[slot, m, 0:8, :] & HI16, F32)
            acc1 = acc1 + gm * lax.bitcast_convert_type(buf[slot, m, 8:16, :] & HI16, F32)
        y_ref[t] = x_ref[t] + jnp.concatenate([acc0, acc1], axis=0)

    def no_issue(lo, hi):
        del lo, hi

    def token(t, j, with_issue):
        issue_next = (lambda lo, hi: issue(t + lookahead, (j + lookahead) % N_SLOTS, lo, hi)) if with_issue else no_issue
        wait((j + 1) % N_SLOTS)
        qs = stage_a_groups(t + 1, (j + 1) % N_SLOTS, issue_next)
        stage_b(t, j, j % 2, issue_next)
        stage_a_tail(t + 1, (j + 1) % 2, qs)

    for t in range(lookahead):
        issue(t, t, 0, N_SEL)
    xb = x_ref[...]
    h_ref[...] = xb * _block_rms_scale(xb) * g_ref[...]
    wait(0)
    stage_a_tail(0, 0, stage_a_groups(0, 0, no_issue))

    n_full = (tb - lookahead) // N_SLOTS

    def steady(i, carry):
        for j in range(N_SLOTS):
            token(i * N_SLOTS + j, j, True)
        return carry

    lax.fori_loop(0, n_full, steady, 0)
    for t in range(n_full * N_SLOTS, tb - 1):
        token(t, t % N_SLOTS, t + lookahead < tb)
    stage_b(tb - 1, (tb - 1) % N_SLOTS, (tb - 1) % 2, no_issue)
    yb = y_ref[...]
    if final_norm:
        yb = yb * _block_rms_scale(yb) * gf_ref[...]
    o_ref[...] = yb.reshape(o_ref.shape)


def _peer_experts(x, eidx, gate, g_ffn, g_final, exp_uv, *, tb, final_norm):
    n = x.shape[0]
    slab = (ROW_SUB, 128)
    d = ROW_SUB * 128
    xspec = pl.BlockSpec((tb,) + slab, lambda i: (i, 0, 0))
    gspec = pl.BlockSpec(slab, lambda i: (0, 0))
    kern = functools.partial(_peer_experts_kernel, tb=tb, final_norm=final_norm)
    return pl.pallas_call(
        kern,
        grid=(n // tb,),
        in_specs=[
            pl.BlockSpec((tb, N_SEL), lambda i: (i, 0), memory_space=pltpu.SMEM),
            xspec,
            pl.BlockSpec((tb, N_SEL), lambda i: (i, 0)),
            gspec,
            gspec,
            pl.BlockSpec(memory_space=pl.ANY),
        ],
        out_specs=pl.BlockSpec((tb, d), lambda i: (i, 0)),
        out_shape=jax.ShapeDtypeStruct((n, d), F32),
        scratch_shapes=[
            pltpu.VMEM((N_SLOTS, N_SEL) + slab, jnp.int32),
            pltpu.SemaphoreType.DMA((N_SLOTS,)),
            pltpu.VMEM((2, N_SEL, 128), F32),
            pltpu.VMEM((tb,) + slab, F32),
            pltpu.VMEM((tb,) + slab, F32),
        ],
        compiler_params=_params("arbitrary"),
        name="peer_experts",
    )(eidx, x, gate, g_ffn.reshape(slab), g_final.reshape(slab), exp_uv)


def _layer(x3, k_past, v_past, sc_past, cc_past, w, g_final, final_norm):
    (g_mix, w_in16, w_sconv, w_cconv, b_cconv, g_cnorm, b_cnorm, g_vnorm, b_vnorm,
     w_spatial, b_spatial, w_out, g_ffn, w_query16, keys1, keys2, exp_uv) = w
    b, t, d = x3.shape
    n = b * t
    x = x3.reshape(n, d)
    tm = min(1024, n)

    proj = _norm_matmul(x, g_mix, w_in16, tm=tm, tn=1280)
    proj3 = proj.reshape(b, t, proj.shape[1])
    k_new = proj3[:, :, G_WIDTH:2 * G_WIDTH]
    v_new = proj3[:, :, 2 * G_WIDTH:3 * G_WIDTH]
    if k_past is None:
        y_a = _sb_attention(proj3, 0, proj3, 1, proj3, 2, t_q=t, tq=128, tk=128, q_offset=0)
    else:
        past = k_past.shape[1]
        k_all = jnp.concatenate([k_past.reshape(b, past, G_WIDTH), k_new], axis=1)
        v_all = jnp.concatenate([v_past.reshape(b, past, G_WIDTH), v_new], axis=1)
        y_a = _sb_attention(proj3, 0, k_all, 0, v_all, 0, t_q=t, tq=t, tk=t, q_offset=past)
    y_bcd, vn, sc_state, cc_state = _mixers(
        proj3, sc_past, cc_past, w_sconv, w_cconv, b_cconv, g_cnorm, b_cnorm, g_vnorm, b_vnorm,
        w_spatial, b_spatial, tt=min(512, t))
    x1 = _out_proj(x, y_a.reshape(n, G_WIDTH), y_bcd.reshape(n, 3 * G_WIDTH), w_out, tm=min(256, n))
    q = _norm_matmul(x1, g_ffn, w_query16, tm=tm, tn=2048)
    eidx, gate = _peer_topk(q, keys1, keys2, tn=min(512, n))
    x2 = _peer_experts(x1, eidx, gate, g_ffn, g_final, exp_uv, tb=min(256, n), final_norm=final_norm)
    heads = (b, t, N_HEADS, HEAD_DIM)
    return x2.reshape(b, t, d), (k_new.reshape(heads), v_new.reshape(heads), sc_state, cc_state, vn)


def kernel(x_prompt, x_sample, cache_k, cache_v, state_sconv, state_cconv, g_mix, w_in, w_sconv, w_cconv,
           b_cconv, g_cnorm, b_cnorm, g_vnorm, b_vnorm, w_spatial, b_spatial, w_out, g_ffn, w_query,
           sub_keys1, sub_keys2, expert_u, expert_v, g_final):
    depth = w_in.shape[0]
    xp, xs = x_prompt, x_sample
    nb = xp.shape[0]
    outs = [[] for _ in range(9)]
    for l in range(depth):
        w = (g_mix[l], w_in[l].astype(BF16), w_sconv[l], w_cconv[l], b_cconv[l], g_cnorm[l], b_cnorm[l],
             g_vnorm[l], b_vnorm[l], w_spatial[l], b_spatial[l], w_out[l], g_ffn[l],
             w_query[l].astype(BF16), sub_keys1[l], sub_keys2[l],
             _pack_experts(expert_u, expert_v, l, tr=256))
        last = l == depth - 1
        sc0 = jnp.zeros((nb, SCONV_W - 1, G_WIDTH), F32)
        cc0 = jnp.zeros((nb, CCONV_W - 1, G_WIDTH), F32)
        xp, (k1, v1, sc1, cc1, _) = _layer(xp, None, None, sc0, cc0, w, g_final, last)
        xs, (k2, v2, sc2, cc2, gv2) = _layer(xs, cache_k[l], cache_v[l], state_sconv[l], state_cconv[l],
                                             w, g_final, last)
        for lst, val in zip(outs, (k1, v1, k2, v2, sc1, sc2, cc1, cc2, gv2)):
            lst.append(val)
    kp, vp, ksm, vsm, scp, scs, ccp, ccs, gvs = [jnp.stack(o) for o in outs]
    return (xp, xs, kp, vp, ksm, vsm, scp, scs, ccp, ccs, gvs)
```

```python
import functools

import jax
import jax.numpy as jnp
from jax import lax
from jax.experimental import pallas as pl
from jax.experimental.pallas import tpu as pltpu

F32 = jnp.float32
BF16 = jnp.bfloat16
EPS = 1e-6

G_WIDTH = 512
N_HEADS = 8
HEAD_DIM = 64
SCONV_W = 3
CCONV_W = 31
MLP_CHUNK = 128
PEER_HEADS = 8
PEER_KEYS = 128
PEER_TOPK = 16
PEER_HALF = 128
N_SEL = PEER_HEADS * PEER_TOPK

VMEM_LIMIT_BYTES = 56 * 1024 * 1024


def _params(*sem):
    return pltpu.CompilerParams(dimension_semantics=sem, vmem_limit_bytes=VMEM_LIMIT_BYTES)


def _gelu_tanh(x):
    return 0.5 * x * (1.0 + jnp.tanh(0.7978845608028654 * (x + 0.044715 * (x * x * x))))


def _layernorm(x, g, b):
    xc = x - jnp.mean(x, axis=-1, keepdims=True)
    var = jnp.mean(xc * xc, axis=-1, keepdims=True)
    return xc * lax.rsqrt(var + EPS) * g + b


def _rmsnorm_rows(x, g):
    return x * lax.rsqrt(jnp.mean(x * x, axis=-1, keepdims=True) + EPS) * g


def _norm_matmul_kernel(x_ref, g_ref, w_ref, o_ref, h_ref):
    @pl.when(pl.program_id(1) == 0)
    def _():
        x = x_ref[...].reshape(h_ref.shape)
        h_ref[...] = _rmsnorm_rows(x, g_ref[...]).astype(BF16)

    o_ref[...] = jnp.dot(h_ref[...], w_ref[...], preferred_element_type=F32)


def _norm_matmul(x, g, w, *, tm, tn):
    n = x.shape[0]
    d, nout = w.shape
    xblock = (tm,) + x.shape[1:]
    return pl.pallas_call(
        _norm_matmul_kernel,
        grid=(n // tm, nout // tn),
        in_specs=[
            pl.BlockSpec(xblock, lambda i, j: (i,) + (0,) * (len(xblock) - 1)),
            pl.BlockSpec((1, d), lambda i, j: (0, 0)),
            pl.BlockSpec((d, tn), lambda i, j: (0, j)),
        ],
        out_specs=pl.BlockSpec((tm, tn), lambda i, j: (i, j)),
        out_shape=jax.ShapeDtypeStruct((n, nout), F32),
        scratch_shapes=[pltpu.VMEM((tm, d), BF16)],
        compiler_params=_params("parallel", "arbitrary"),
        name="norm_matmul",
    )(x, g.reshape(1, d), w)


RUN_CUTOFF = -104.0


def _sb_kernel(q_ref, k_ref, v_ref, o_ref, run_ref, acc_ref, q16_ref, *, tq, tk, q_offset):
    qi = pl.program_id(1)
    q_pos0 = q_offset + qi * tq
    n_kb = jnp.minimum(pl.cdiv(q_pos0 + tq, tk), k_ref.shape[1] // tk)

    run_ref[...] = jnp.zeros_like(run_ref)
    acc_ref[...] = jnp.zeros_like(acc_ref)
    for h in range(N_HEADS):
        q16_ref[h] = (q_ref[0, :, h * HEAD_DIM:(h + 1) * HEAD_DIM] * (HEAD_DIM ** -0.5)).astype(BF16)

    rows = N_HEADS * tq
    row = lax.broadcasted_iota(jnp.int32, (rows, tk), 0) & (tq - 1)
    col_minus_row = lax.broadcasted_iota(jnp.int32, (rows, tk), 1) - row
    col = lax.broadcasted_iota(jnp.int32, (2 * tk, 2 * tk), 1)
    krow = lax.broadcasted_iota(jnp.int32, (2 * tk, 2 * tk), 0) & (tk - 1)
    suffix_and_total = jnp.where((krow >= col) | (col >= tk), 1.0, 0.0).astype(BF16)

    def body(carry):
        it, _ = carry
        kb = n_kb - 1 - it
        k0 = pl.multiple_of(kb * tk, tk)
        mask = col_minus_row < (q_pos0 - k0)
        zs = []
        for h in range(N_HEADS):
            kh = k_ref[0, pl.ds(k0, tk), h * HEAD_DIM:(h + 1) * HEAD_DIM].astype(BF16)
            zs.append(lax.dot_general(q16_ref[h], kh, (((1,), (1,)), ((), ())), preferred_element_type=F32))
        z = jnp.concatenate(zs, axis=0)
        lf = -(jnp.maximum(z, 0.0) + jnp.log1p(jnp.exp(-jnp.abs(z))))
        lf = jnp.where(mask, lf, 0.0)
        lf_hi = lf.astype(BF16)
        lf_lo = (lf - lf_hi.astype(F32)).astype(BF16)
        sums = jnp.dot(jnp.concatenate([lf_hi, lf_lo], axis=1), suffix_and_total, preferred_element_type=F32)
        run = run_ref[...]
        w = jnp.where(mask, jnp.exp(z + sums[:, :tk] + run), 0.0).astype(BF16)
        for h in range(N_HEADS):
            vh = v_ref[0, pl.ds(k0, tk), h * HEAD_DIM:(h + 1) * HEAD_DIM].astype(BF16)
            acc_ref[h] += jnp.dot(w[h * tq:(h + 1) * tq], vh, preferred_element_type=F32)
        run = run + sums[:, tk:]
        run_ref[...] = run
        return it + 1, jnp.max(run)

    lax.while_loop(lambda c: (c[0] < n_kb) & (c[1] > RUN_CUTOFF), body, (0, 0.0))
    o_ref[0] = jnp.concatenate([acc_ref[h] for h in range(N_HEADS)], axis=-1)


def _sb_attention(q_arr, q_col, k_arr, k_col, v_arr, v_col, *, t_q, tq, tk, q_offset):
    b = q_arr.shape[0]
    t_k = k_arr.shape[1]
    kern = functools.partial(_sb_kernel, tq=tq, tk=tk, q_offset=q_offset)
    return pl.pallas_call(
        kern,
        grid=(b, t_q // tq),
        in_specs=[
            pl.BlockSpec((1, tq, G_WIDTH), lambda i, j: (i, j, q_col)),
            pl.BlockSpec((1, t_k, G_WIDTH), lambda i, j: (i, 0, k_col)),
            pl.BlockSpec((1, t_k, G_WIDTH), lambda i, j: (i, 0, v_col)),
        ],
        out_specs=pl.BlockSpec((1, tq, G_WIDTH), lambda i, j: (i, j, 0)),
        out_shape=jax.ShapeDtypeStruct((b, t_q, G_WIDTH), F32),
        scratch_shapes=[
            pltpu.VMEM((N_HEADS * tq, tk), F32),
            pltpu.VMEM((N_HEADS, tq, HEAD_DIM), F32),
            pltpu.VMEM((N_HEADS, tq, HEAD_DIM), BF16),
        ],
        compiler_params=_params("parallel", "arbitrary"),
        name="sb_attention",
    )(q_arr, k_arr, v_arr)


SC_HIST = 8
CC_HIST = 32


def _mixers_kernel(scb_ref, scc_ref, sch_ref, cfa_ref, cfg_ref, gmu_ref, gmv_ref,
                   sc_past_ref, cc_past_ref, wsc_ref, wcc_ref, bcc_ref, gcn_ref, bcn_ref,
                   gvn_ref, bvn_ref, wsp_ref, bsp_ref,
                   y_ref, vn_ref, sc_state_ref, cc_state_ref, u_buf, g_buf, *, tt, chunk):
    ti = pl.program_id(1)

    @pl.when(ti == 0)
    def _():
        u_buf[0:SC_HIST, :] = jnp.zeros((SC_HIST, G_WIDTH), F32)
        g_buf[0:CC_HIST, :] = jnp.zeros((CC_HIST, G_WIDTH), F32)
        u_buf[SC_HIST - (SCONV_W - 1):SC_HIST, :] = sc_past_ref[0]
        g_buf[CC_HIST - (CCONV_W - 1):CC_HIST, :] = cc_past_ref[0]

    u_buf[SC_HIST:SC_HIST + tt, :] = scc_ref[0] * sch_ref[0]
    conv_b = jnp.zeros((tt, G_WIDTH), F32)
    for k in range(SCONV_W):
        s0 = SC_HIST - (SCONV_W - 1) + k
        conv_b = conv_b + wsc_ref[k:k + 1, :] * u_buf[s0:s0 + tt, :]
    y_ref[0, :, 0:G_WIDTH] = scb_ref[0] * conv_b

    g_buf[CC_HIST:CC_HIST + tt, :] = cfa_ref[0] * jax.nn.sigmoid(cfg_ref[0])
    conv_c = jnp.zeros((tt, G_WIDTH), F32)
    for k in range(CCONV_W):
        s0 = CC_HIST - (CCONV_W - 1) + k
        conv_c = conv_c + wcc_ref[k:k + 1, :] * g_buf[s0:s0 + tt, :]
    ln = _layernorm(conv_c + bcc_ref[...], gcn_ref[...], bcn_ref[...])
    y_ref[0, :, G_WIDTH:2 * G_WIDTH] = ln * jax.nn.sigmoid(ln)

    vn = _layernorm(_gelu_tanh(gmv_ref[0]), gvn_ref[...], bvn_ref[...])
    vn_ref[0] = vn
    vn16 = vn.astype(BF16)
    for c in range(tt // chunk):
        rows = slice(c * chunk, (c + 1) * chunk)
        parts = []
        for h in range(N_HEADS):
            hs = slice(h * HEAD_DIM, (h + 1) * HEAD_DIM)
            parts.append(jnp.dot(wsp_ref[h], vn16[rows, hs], preferred_element_type=F32))
        s = jnp.concatenate(parts, axis=-1) + bsp_ref[...]
        y_ref[0, rows, 2 * G_WIDTH:3 * G_WIDTH] = _gelu_tanh(gmu_ref[0, rows, :]) * s

    sc_state_ref[0] = u_buf[SC_HIST + tt - (SCONV_W - 1):SC_HIST + tt, :]
    cc_state_ref[0] = g_buf[CC_HIST + tt - (CCONV_W - 1):CC_HIST + tt, :]
    u_buf[0:SC_HIST, :] = u_buf[tt:tt + SC_HIST, :]
    g_buf[0:CC_HIST, :] = g_buf[tt:tt + CC_HIST, :]


def _mixers(proj3, sc_past, cc_past, w_sconv, w_cconv, b_cconv, g_cnorm, b_cnorm, g_vnorm, b_vnorm,
            w_spatial, b_spatial, *, tt):
    b, t, _ = proj3.shape
    chunk = min(t, MLP_CHUNK)
    wsp = jnp.tril(w_spatial[:, :chunk, :chunk]).astype(BF16)
    bsp = jnp.repeat(b_spatial[:, :chunk].T, HEAD_DIM, axis=1)

    def col(c):
        return pl.BlockSpec((1, tt, G_WIDTH), lambda i, j: (i, j, c))

    def whole(a):
        return pl.BlockSpec(a.shape, lambda i, j: (0,) * a.ndim)

    def per_batch(a):
        return pl.BlockSpec((1,) + a.shape[1:], lambda i, j: (i,) + (0,) * (a.ndim - 1))

    row = lambda a: a.reshape(1, G_WIDTH)
    small = [w_sconv, w_cconv, row(b_cconv), row(g_cnorm), row(b_cnorm), row(g_vnorm), row(b_vnorm), wsp, bsp]
    kern = functools.partial(_mixers_kernel, tt=tt, chunk=chunk)
    return pl.pallas_call(
        kern,
        grid=(b, t // tt),
        in_specs=[col(c) for c in range(3, 10)] + [per_batch(sc_past), per_batch(cc_past)]
                 + [whole(a) for a in small],
        out_specs=[
            pl.BlockSpec((1, tt, 3 * G_WIDTH), lambda i, j: (i, j, 0)),
            pl.BlockSpec((1, tt, G_WIDTH), lambda i, j: (i, j, 0)),
            pl.BlockSpec((1, SCONV_W - 1, G_WIDTH), lambda i, j: (i, 0, 0)),
            pl.BlockSpec((1, CCONV_W - 1, G_WIDTH), lambda i, j: (i, 0, 0)),
        ],
        out_shape=[
            jax.ShapeDtypeStruct((b, t, 3 * G_WIDTH), F32),
            jax.ShapeDtypeStruct((b, t, G_WIDTH), F32),
            jax.ShapeDtypeStruct((b, SCONV_W - 1, G_WIDTH), F32),
            jax.ShapeDtypeStruct((b, CCONV_W - 1, G_WIDTH), F32),
        ],
        scratch_shapes=[pltpu.VMEM((SC_HIST + tt, G_WIDTH), F32), pltpu.VMEM((CC_HIST + tt, G_WIDTH), F32)],
        compiler_params=_params("parallel", "arbitrary"),
        name="mixers",
    )(*([proj3] * 7), sc_past, cc_past, *small)


def _out_proj_kernel(x_ref, ya_ref, yb_ref, wa_ref, wb_ref, o_ref):
    y = (x_ref[...]
         + jnp.dot(ya_ref[...].astype(BF16), wa_ref[...], preferred_element_type=F32)
         + jnp.dot(yb_ref[...].astype(BF16), wb_ref[...], preferred_element_type=F32))
    o_ref[...] = y.reshape(o_ref.shape)


def _out_proj(x, ya, yb, w_out, *, tm):
    n, d = x.shape
    slab = (d // 128, 128)
    wa = w_out[:G_WIDTH].astype(BF16)
    wb = w_out[G_WIDTH:].astype(BF16)
    return pl.pallas_call(
        _out_proj_kernel,
        grid=(n // tm,),
        in_specs=[
            pl.BlockSpec((tm, d), lambda i: (i, 0)),
            pl.BlockSpec((tm, G_WIDTH), lambda i: (i, 0)),
            pl.BlockSpec((tm, 3 * G_WIDTH), lambda i: (i, 0)),
            pl.BlockSpec(wa.shape, lambda i: (0, 0)),
            pl.BlockSpec(wb.shape, lambda i: (0, 0)),
        ],
        out_specs=pl.BlockSpec((tm,) + slab, lambda i: (i, 0, 0)),
        out_shape=jax.ShapeDtypeStruct((n,) + slab, F32),
        compiler_params=_params("parallel"),
        name="out_proj",
    )(x, ya, yb, wa, wb)


def _top16_rows(s, rid, rid_end):
    vals, idxs = [], []
    for _ in range(PEER_TOPK):
        m = jnp.max(s, axis=0, keepdims=True)
        i = jnp.min(jnp.where(s == m, rid, rid_end), axis=0, keepdims=True)
        s = jnp.where(rid == i, -jnp.inf, s)
        vals.append(m)
        idxs.append(i)
    return jnp.concatenate(vals, axis=0), jnp.concatenate(idxs, axis=0)


CAND_B = 8
CAND_ROWS = PEER_TOPK + (CAND_B - 1) * CAND_B + (PEER_TOPK - CAND_B)


def _pair_rows(x1, x2, combine):
    rows = [combine(x1[0:1], x2)]
    rows += [combine(x1[a:a + 1], x2[0:CAND_B]) for a in range(1, CAND_B)]
    rows.append(combine(x1[CAND_B:], x2[0:1]))
    return jnp.concatenate(rows, axis=0)


def _peer_topk_kernel(q_ref, k1_ref, k2_ref, eidx_ref, gate_ref):
    nt = (((1,), (1,)), ((), ()))
    tn = q_ref.shape[0]
    key_id = lax.broadcasted_iota(jnp.int32, (PEER_KEYS, tn), 0).astype(F32)
    r = lax.broadcasted_iota(jnp.int32, (CAND_ROWS, tn), 0)
    mid = r - PEER_TOPK
    last = r - (CAND_ROWS - (PEER_TOPK - CAND_B))
    flat_id = jnp.where(r < PEER_TOPK, r,
                        jnp.where(last < 0, ((mid >> 3) + 1) * PEER_TOPK + (mid & (CAND_B - 1)),
                                  (last + CAND_B) * PEER_TOPK)).astype(F32)
    assert CAND_B == 8
    e_rows, g_rows = [], []
    for h in range(PEER_HEADS):
        qa = q_ref[:, (2 * h) * PEER_HALF:(2 * h + 1) * PEER_HALF].astype(BF16)
        qb = q_ref[:, (2 * h + 1) * PEER_HALF:(2 * h + 2) * PEER_HALF].astype(BF16)
        s1 = lax.dot_general(k1_ref[h].astype(BF16), qa, nt, preferred_element_type=F32)
        s2 = lax.dot_general(k2_ref[h].astype(BF16), qb, nt, preferred_element_type=F32)
        t1, i1 = _top16_rows(s1, key_id, float(PEER_KEYS))
        t2, i2 = _top16_rows(s2, key_id, float(PEER_KEYS))
        cand = _pair_rows(t1, t2, lambda a, b: a + b)
        cidx = _pair_rows(i1, i2, lambda a, b: a * float(PEER_KEYS) + b)
        top, sel = _top16_rows(cand, flat_id, float(PEER_TOPK * PEER_TOPK))
        eidx = [jnp.sum(jnp.where(flat_id == sel[r:r + 1], cidx, 0.0), axis=0, keepdims=True)
                for r in range(PEER_TOPK)]
        p = jnp.exp(top - top[0:1])
        g_rows.append(p / jnp.sum(p, axis=0, keepdims=True))
        e_rows.append(jnp.concatenate(eidx, axis=0))
    eidx_ref[...] = jnp.concatenate(e_rows, axis=0).T.astype(jnp.int32)
    gate_ref[...] = jnp.concatenate(g_rows, axis=0).T


def _peer_topk(q, keys1, keys2, *, tn):
    n, d = q.shape
    return pl.pallas_call(
        _peer_topk_kernel,
        grid=(n // tn,),
        in_specs=[
            pl.BlockSpec((tn, d), lambda i: (i, 0)),
            pl.BlockSpec(keys1.shape, lambda i: (0, 0, 0)),
            pl.BlockSpec(keys2.shape, lambda i: (0, 0, 0)),
        ],
        out_specs=[pl.BlockSpec((tn, N_SEL), lambda i: (i, 0)), pl.BlockSpec((tn, N_SEL), lambda i: (i, 0))],
        out_shape=[jax.ShapeDtypeStruct((n, N_SEL), jnp.int32), jax.ShapeDtypeStruct((n, N_SEL), F32)],
        compiler_params=_params("parallel"),
        name="peer_topk",
    )(q, keys1, keys2)


N_SLOTS = 8
HI16 = -65536
ROW_SUB = 16


def _pack_experts_kernel(u_ref, v_ref, o_ref):
    ub = lax.bitcast_convert_type(u_ref[...].astype(BF16).astype(F32), jnp.int32)
    vb = lax.bitcast_convert_type(v_ref[...].astype(BF16).astype(F32), jnp.int32)
    o_ref[...] = (lax.shift_right_logical(ub, 16) | (vb & HI16)).reshape(o_ref.shape)


def _pack_experts(exp_u, exp_v, layer, *, tr):
    _, n, d = exp_u.shape
    spec = pl.BlockSpec((None, tr, d), lambda i: (layer, i, 0))
    return pl.pallas_call(
        _pack_experts_kernel,
        grid=(n // tr,),
        in_specs=[spec, spec],
        out_specs=pl.BlockSpec((tr, ROW_SUB, 128), lambda i: (i, 0, 0)),
        out_shape=jax.ShapeDtypeStruct((n, ROW_SUB, 128), jnp.int32),
        compiler_params=_params("parallel"),
        name="pack_experts",
    )(exp_u, exp_v)


def _sublane_sums8(ps):
    sid = lax.broadcasted_iota(jnp.int32, (8, 128), 0)
    m4 = sid < 4
    m2 = (sid & 2) == 0
    m1 = (sid & 1) == 0
    b = [jnp.where(m4, ps[k], ps[k + 4]) + pltpu.roll(jnp.where(m4, ps[k + 4], ps[k]), 4, 0) for k in range(4)]
    c = [jnp.where(m2, b[k], b[k + 2]) + jnp.where(m2, pltpu.roll(b[k], 6, 0), pltpu.roll(b[k + 2], 2, 0))
         for k in range(2)]
    return jnp.where(m1, c[0], c[1]) + jnp.where(m1, pltpu.roll(c[0], 7, 0), pltpu.roll(c[1], 1, 0))


def _block_rms_scale(x):
    ss = jnp.sum(jnp.sum(x * x, axis=2, keepdims=True), axis=1, keepdims=True)
    return lax.rsqrt(ss * (1.0 / (ROW_SUB * 128)) + EPS)


def _peer_experts_kernel(eidx_ref, x_ref, gate_ref, g_ref, gf_ref, uv_hbm, o_ref, buf, sem, ga_ref, h_ref, y_ref,
                         *, tb, final_norm):
    n_groups = N_SEL // 8
    per = N_SEL // (2 * n_groups)
    lookahead = N_SLOTS - 1

    def issue(t, slot, lo, hi):
        for m in range(lo, hi):
            e = eidx_ref[t, m]
            pltpu.make_async_copy(uv_hbm.at[pl.ds(e, 1)], buf.at[slot, pl.ds(m, 1)],
                                  sem.at[slot]).start(priority=m % 2)

    def wait(slot):
        pltpu.make_async_copy(uv_hbm.at[pl.ds(0, N_SEL)], buf.at[slot], sem.at[slot]).wait()

    eye = (lax.broadcasted_iota(jnp.int32, (N_SEL, N_SEL), 0)
           == lax.broadcasted_iota(jnp.int32, (N_SEL, N_SEL), 1))

    def stage_a_groups(t, slot, issue_next):
        h0, h1 = h_ref[t, 0:8, :], h_ref[t, 8:16, :]
        qs = []
        for gi in range(n_groups):
            issue_next(gi * per, (gi + 1) * per)
            ps = []
            for j in range(8):
                m = gi * 8 + j
                u0 = lax.bitcast_convert_type(lax.shift_left(buf[slot, m, 0:8, :], 16), F32)
                u1 = lax.bitcast_convert_type(lax.shift_left(buf[slot, m, 8:16, :], 16), F32)
                ps.append(u0 * h0 + u1 * h1)
            qs.append(_sublane_sums8(ps))
        return qs

    def stage_a_tail(t, par, qs):
        s_row = jnp.sum(jnp.concatenate(qs, axis=0).T, axis=0, keepdims=True)
        ga_row = gate_ref[pl.ds(t, 1), :] * _gelu_tanh(s_row)
        ga_ref[par] = jnp.broadcast_to(ga_row, (N_SEL, N_SEL)).T

    def stage_b(t, slot, par, issue_next):
        acc0 = jnp.zeros((8, 128), F32)
        acc1 = jnp.zeros((8, 128), F32)
        for m in range(N_SEL):
            if m % 8 == 0:
                gi = m // 8
                issue_next(N_SEL // 2 + gi * per, N_SEL // 2 + (gi + 1) * per)
            gm = jnp.broadcast_to(ga_ref[par, pl.ds(m, 1), :], (8, 128))
            acc0 = acc0 + gm * lax.bitcast_convert_type(buf[slot, m, 0:8, :] & HI16, F32)
            acc1 = acc1 + gm * lax.bitcast_convert_type(buf[slot, m, 8:16, :] & HI16, F32)
        y_ref[t] = x_ref[t] + jnp.concatenate([acc0, acc1], axis=0)

    def no_issue(lo, hi):
        del lo, hi

    def token(t, j, with_issue):
        issue_next = (lambda lo, hi: issue(t + lookahead, (j + lookahead) % N_SLOTS, lo, hi)) if with_issue else no_issue
        wait((j + 1) % N_SLOTS)
        qs = stage_a_groups(t + 1, (j + 1) % N_SLOTS, issue_next)
        stage_b(t, j, j % 2, issue_next)
        stage_a_tail(t + 1, (j + 1) % 2, qs)

    for t in range(lookahead):
        issue(t, t, 0, N_SEL)
    xb = x_ref[...]
    h_ref[...] = xb * _block_rms_scale(xb) * g_ref[...]
    wait(0)
    stage_a_tail(0, 0, stage_a_groups(0, 0, no_issue))

    n_full = (tb - lookahead) // N_SLOTS

    def steady(i, carry):
        for j in range(N_SLOTS):
            token(i * N_SLOTS + j, j, True)
        return carry

    lax.fori_loop(0, n_full, steady, 0)
    for t in range(n_full * N_SLOTS, tb - 1):
        token(t, t % N_SLOTS, t + lookahead < tb)
    stage_b(tb - 1, (tb - 1) % N_SLOTS, (tb - 1) % 2, no_issue)
    yb = y_ref[...]
    if final_norm:
        yb = yb * _block_rms_scale(yb) * gf_ref[...]
    o_ref[...] = yb.reshape(o_ref.shape)


def _peer_experts(x, eidx, gate, g_ffn, g_final, exp_uv, *, tb, final_norm):
    n = x.shape[0]
    slab = (ROW_SUB, 128)
    d = ROW_SUB * 128
    xspec = pl.BlockSpec((tb,) + slab, lambda i: (i, 0, 0))
    gspec = pl.BlockSpec(slab, lambda i: (0, 0))
    kern = functools.partial(_peer_experts_kernel, tb=tb, final_norm=final_norm)
    return pl.pallas_call(
        kern,
        grid=(n // tb,),
        in_specs=[
            pl.BlockSpec((tb, N_SEL), lambda i: (i, 0), memory_space=pltpu.SMEM),
            xspec,
            pl.BlockSpec((tb, N_SEL), lambda i: (i, 0)),
            gspec,
            gspec,
            pl.BlockSpec(memory_space=pl.ANY),
        ],
        out_specs=pl.BlockSpec((tb, d), lambda i: (i, 0)),
        out_shape=jax.ShapeDtypeStruct((n, d), F32),
        scratch_shapes=[
            pltpu.VMEM((N_SLOTS, N_SEL) + slab, jnp.int32),
            pltpu.SemaphoreType.DMA((N_SLOTS,)),
            pltpu.VMEM((2, N_SEL, 128), F32),
            pltpu.VMEM((tb,) + slab, F32),
            pltpu.VMEM((tb,) + slab, F32),
        ],
        compiler_params=_params("arbitrary"),
        name="peer_experts",
    )(eidx, x, gate, g_ffn.reshape(slab), g_final.reshape(slab), exp_uv)


def _layer(x3, k_past, v_past, sc_past, cc_past, w, g_final, final_norm):
    (g_mix, w_in16, w_sconv, w_cconv, b_cconv, g_cnorm, b_cnorm, g_vnorm, b_vnorm,
     w_spatial, b_spatial, w_out, g_ffn, w_query16, keys1, keys2, exp_uv) = w
    b, t, d = x3.shape
    n = b * t
    x = x3.reshape(n, d)
    tm = min(1024, n)

    proj = _norm_matmul(x, g_mix, w_in16, tm=tm, tn=1280)
    proj3 = proj.reshape(b, t, proj.shape[1])
    k_new = proj3[:, :, G_WIDTH:2 * G_WIDTH]
    v_new = proj3[:, :, 2 * G_WIDTH:3 * G_WIDTH]
    if k_past is None:
        y_a = _sb_attention(proj3, 0, proj3, 1, proj3, 2, t_q=t, tq=128, tk=128, q_offset=0)
    else:
        past = k_past.shape[1]
        k_all = jnp.concatenate([k_past.reshape(b, past, G_WIDTH), k_new], axis=1)
        v_all = jnp.concatenate([v_past.reshape(b, past, G_WIDTH), v_new], axis=1)
        y_a = _sb_attention(proj3, 0, k_all, 0, v_all, 0, t_q=t, tq=t, tk=t, q_offset=past)
    y_bcd, vn, sc_state, cc_state = _mixers(
        proj3, sc_past, cc_past, w_sconv, w_cconv, b_cconv, g_cnorm, b_cnorm, g_vnorm, b_vnorm,
        w_spatial, b_spatial, tt=min(512, t))
    x1 = _out_proj(x, y_a.reshape(n, G_WIDTH), y_bcd.reshape(n, 3 * G_WIDTH), w_out, tm=min(256, n))
    q = _norm_matmul(x1, g_ffn, w_query16, tm=tm, tn=2048)
    eidx, gate = _peer_topk(q, keys1, keys2, tn=min(512, n))
    x2 = _peer_experts(x1, eidx, gate, g_ffn, g_final, exp_uv, tb=min(256, n), final_norm=final_norm)
    heads = (b, t, N_HEADS, HEAD_DIM)
    return x2.reshape(b, t, d), (k_new.reshape(heads), v_new.reshape(heads), sc_state, cc_state, vn)


def kernel(x_prompt, x_sample, cache_k, cache_v, state_sconv, state_cconv, g_mix, w_in, w_sconv, w_cconv,
           b_cconv, g_cnorm, b_cnorm, g_vnorm, b_vnorm, w_spatial, b_spatial, w_out, g_ffn, w_query,
           sub_keys1, sub_keys2, expert_u, expert_v, g_final):
    depth = w_in.shape[0]
    xp, xs = x_prompt, x_sample
    nb = xp.shape[0]
    outs = [[] for _ in range(9)]
    for l in range(depth):
        w = (g_mix[l], w_in[l].astype(BF16), w_sconv[l], w_cconv[l], b_cconv[l], g_cnorm[l], b_cnorm[l],
             g_vnorm[l], b_vnorm[l], w_spatial[l], b_spatial[l], w_out[l], g_ffn[l],
             w_query[l].astype(BF16), sub_keys1[l], sub_keys2[l],
             _pack_experts(expert_u, expert_v, l, tr=256))
        last = l == depth - 1
        sc0 = jnp.zeros((nb, SCONV_W - 1, G_WIDTH), F32)
        cc0 = jnp.zeros((nb, CCONV_W - 1, G_WIDTH), F32)
        xp, (k1, v1, sc1, cc1, _) = _layer(xp, None, None, sc0, cc0, w, g_final, last)
        xs, (k2, v2, sc2, cc2, gv2) = _layer(xs, cache_k[l], cache_v[l], state_sconv[l], state_cconv[l],
                                             w, g_final, last)
        for lst, val in zip(outs, (k1, v1, k2, v2, sc1, sc2, cc1, cc2, gv2)):
            lst.append(val)
    kp, vp, ksm, vsm, scp, scs, ccp, ccs, gvs = [jnp.stack(o) for o in outs]
    return (xp, xs, kp, vp, ksm, vsm, scp, scs, ccp, ccs, gvs)
```
